```python
import math
import jax, jax.numpy as jnp
from jax import lax
import numpy as np

D_MODEL = 2048
BATCH = 4
SEQ = 2048
DEPTH = 4
DEC_BATCH = 8
DEC_SEQ = 1
PAST_LEN = 16384
PAGE_SIZE = 128

D_ATT = D_MODEL // 2
HEAD_DIM = 64
N_HEADS = D_ATT // HEAD_DIM
D_SSM = D_MODEL // 4
SSM_GROUP = 16
N_SSM_GROUPS = D_SSM // SSM_GROUP
SSM_STATE = 64
D_CONV = D_MODEL - D_ATT - D_SSM
CONV_WIDTH = 31
D_FF = 4 * D_MODEL
Q_BLOCK = 128
D_IN = 3 * D_ATT + D_SSM + 2 * D_CONV
DN_ALPHA = (2 * DEPTH) ** 0.25
DN_BETA = (8 * DEPTH) ** -0.25
LN_EPS = 1e-5
SB_BIAS_INIT = -8.0

kernel_name = 'hybrid_sb_s5_conformer_decoder_step'


def layer_norm(x, g, b):
    x32 = x.astype(jnp.float32)
    mu = jnp.mean(x32, axis=-1, keepdims=True)
    var = jnp.mean(jnp.square(x32 - mu), axis=-1, keepdims=True)
    y = (x32 - mu) * lax.rsqrt(var + LN_EPS)
    return (y * g.astype(jnp.float32) + b.astype(jnp.float32)).astype(x.dtype)


def stick_breaking(q, k, v, bias, q_offset):
    tq = q.shape[1]
    scale = HEAD_DIM ** -0.5
    bias = bias.astype(jnp.float32)[None, :, None, None]
    outs = []
    for start in range(0, tq, Q_BLOCK):
        end = min(start + Q_BLOCK, tq)
        kend = q_offset + end
        qb = q[:, start:end]
        kb = k[:, :kend]
        vb = v[:, :kend]
        z = jnp.einsum('bqhd,bkhd->bhqk', qb, kb).astype(jnp.float32) * scale + bias
        qpos = q_offset + start + jnp.arange(end - start)
        kpos = jnp.arange(kend)
        mask = (kpos[None, :] < qpos[:, None])[None, None]
        log_beta = jax.nn.log_sigmoid(z)
        log_keep = jnp.where(mask, jax.nn.log_sigmoid(-z), 0.0)
        tail = lax.cumsum(log_keep, axis=3, reverse=True) - log_keep
        w = jnp.where(mask, jnp.exp(log_beta + tail), 0.0)
        outs.append(jnp.einsum('bhqk,bkhd->bqhd', w.astype(v.dtype), vb))
    return jnp.concatenate(outs, axis=1)


def _complex_affine_combine(e1, e2):
    a1r, a1i, b1r, b1i = e1
    a2r, a2i, b2r, b2i = e2
    ar = a2r * a1r - a2i * a1i
    ai = a2r * a1i + a2i * a1r
    br = a2r * b1r - a2i * b1i + b2r
    bi = a2r * b1i + a2i * b1r + b2i
    return (ar, ai, br, bi)


def s5_mixer(u, p, h0_re, h0_im):
    bsz, t, _ = u.shape
    u = u.reshape(bsz, t, N_SSM_GROUPS, SSM_GROUP)
    dt = jnp.exp(p['ssm_log_dt'])[:, None]
    lr, li = p['ssm_lam_re'], p['ssm_lam_im']
    mag = jnp.exp(lr * dt)
    ang = li * dt
    ab_re = mag * jnp.cos(ang)
    ab_im = mag * jnp.sin(ang)
    den = lr * lr + li * li
    nr = ab_re - 1.0
    f_re = (nr * lr + ab_im * li) / den
    f_im = (ab_im * lr - nr * li) / den
    bb_re = f_re[..., None] * p['ssm_b_re'] - f_im[..., None] * p['ssm_b_im']
    bb_im = f_re[..., None] * p['ssm_b_im'] + f_im[..., None] * p['ssm_b_re']
    bu_re = jnp.einsum('btgc,gpc->btgp', u, bb_re)
    bu_im = jnp.einsum('btgc,gpc->btgp', u, bb_im)
    if h0_re is not None:
        h0_re = h0_re.astype(bu_re.dtype)
        h0_im = h0_im.astype(bu_re.dtype)
        bu_re = bu_re.at[:, 0].add(ab_re * h0_re - ab_im * h0_im)
        bu_im = bu_im.at[:, 0].add(ab_re * h0_im + ab_im * h0_re)
    a_re = jnp.broadcast_to(ab_re, bu_re.shape)
    a_im = jnp.broadcast_to(ab_im, bu_im.shape)
    _, _, x_re, x_im = lax.associative_scan(_complex_affine_combine, (a_re, a_im, bu_re, bu_im), axis=1)
    y = (jnp.einsum('btgp,gcp->btgc', x_re, p['ssm_c_re'])
         - jnp.einsum('btgp,gcp->btgc', x_im, p['ssm_c_im'])
         + p['ssm_d'].reshape(N_SSM_GROUPS, SSM_GROUP) * u)
    y = jax.nn.gelu(y.reshape(bsz, t, D_SSM))
    y = y * jax.nn.sigmoid(y @ p['ssm_w_glu'] + p['ssm_b_glu'])
    return y, x_re[:, -1], x_im[:, -1]


def conformer_conv(a, b, buf, p):
    g = a * jax.nn.sigmoid(b)
    if buf is None:
        buf = jnp.zeros((g.shape[0], CONV_WIDTH - 1, D_CONV), g.dtype)
    gp = jnp.concatenate([buf.astype(g.dtype), g], axis=1)
    y = lax.conv_general_dilated(gp, p['conv_w'][:, None, :].astype(g.dtype), window_strides=(1,),
                                 padding='VALID', dimension_numbers=('NWC', 'WIO', 'NWC'),
                                 feature_group_count=D_CONV) + p['conv_b']
    y = jax.nn.silu(layer_norm(y, p['conv_ln_g'], p['conv_ln_b']))
    return y, gp[:, -(CONV_WIDTH - 1):]


def trunk_layer(x, c, p, k_past, v_past, h0_re, h0_im, conv_buf, q_offset):
    bsz, t, _ = x.shape
    mod = jax.nn.silu(c) @ p['w_ada'] + p['b_ada']
    sh1, sc1, g1, sh2, sc2, g2 = jnp.split(mod[:, None, :], 6, axis=-1)
    h = x * (1.0 + sc1) + sh1
    proj = h @ p['w_in']
    q, k, v, u, ca, cb = jnp.split(proj, [D_ATT, 2 * D_ATT, 3 * D_ATT, 3 * D_ATT + D_SSM,
                                          3 * D_ATT + D_SSM + D_CONV], axis=-1)
    q = q.reshape(bsz, t, N_HEADS, HEAD_DIM)
    k = k.reshape(bsz, t, N_HEADS, HEAD_DIM)
    v = v.reshape(bsz, t, N_HEADS, HEAD_DIM)
    if k_past is None:
        k_all, v_all = k, v
    else:
        k_all = jnp.concatenate([k_past.astype(k.dtype), k], axis=1)
        v_all = jnp.concatenate([v_past.astype(v.dtype), v], axis=1)
    att = stick_breaking(q, k_all, v_all, p['sb_bias'], q_offset).reshape(bsz, t, D_ATT)
    y_ssm, s_re, s_im = s5_mixer(u, p, h0_re, h0_im)
    y_conv, new_buf = conformer_conv(ca, cb, conv_buf, p)
    mix = jnp.concatenate([att, y_ssm, y_conv], axis=-1) @ p['w_out']
    x = layer_norm(DN_ALPHA * x + g1 * mix, p['ln1_g'], p['ln1_b'])
    h2 = x * (1.0 + sc2) + sh2
    ff = jnp.square(jax.nn.relu(h2 @ p['w_ff1'])) @ p['w_ff2']
    x = layer_norm(DN_ALPHA * x + g2 * ff, p['ln2_g'], p['ln2_b'])
    return x, k, v, s_re, s_im, new_buf


def setup_inputs(seed: int = 0) -> dict:
    key = jax.random.key(seed)
    ks = jax.random.split(key, 40)
    f32 = jnp.float32
    n_pages = PAST_LEN // PAGE_SIZE
    n_used = DEC_BATCH * n_pages
    n_pool = n_used + max(n_used // 4, 1)
    nrm = lambda i, shape, s: jax.random.normal(ks[i], shape, f32) * s
    page_table = jax.random.permutation(ks[0], n_pool)[:n_used].reshape(DEC_BATCH, n_pages).astype(jnp.int32)
    lam_im = jnp.pi * jnp.arange(SSM_STATE, dtype=f32)[None, None, :] + nrm(21, (DEPTH, N_SSM_GROUPS, SSM_STATE), 0.01)
    return {
        'x_prompt': nrm(1, (BATCH, SEQ, D_MODEL), 1.0),
        'x_sample': nrm(2, (DEC_BATCH, DEC_SEQ, D_MODEL), 1.0),
        'cache_k': nrm(3, (DEPTH, n_pool, PAGE_SIZE, N_HEADS, HEAD_DIM), 1.0),
        'cache_v': nrm(4, (DEPTH, n_pool, PAGE_SIZE, N_HEADS, HEAD_DIM), 1.0),
        'state_ssm_re': nrm(5, (DEPTH, DEC_BATCH, N_SSM_GROUPS, SSM_STATE), 0.5),
        'state_ssm_im': nrm(6, (DEPTH, DEC_BATCH, N_SSM_GROUPS, SSM_STATE), 0.5),
        'state_conv': nrm(7, (DEPTH, DEC_BATCH, CONV_WIDTH - 1, D_CONV), 0.5),
        'page_table': page_table,
        'c_prompt': nrm(8, (BATCH, D_MODEL), 1.0),
        'c_sample': nrm(9, (DEC_BATCH, D_MODEL), 1.0),
        'w_ada': nrm(10, (DEPTH, D_MODEL, 6 * D_MODEL), 0.5 * D_MODEL ** -0.5),
        'b_ada': nrm(11, (DEPTH, 6 * D_MODEL), 0.02),
        'w_in': nrm(12, (DEPTH, D_MODEL, D_IN), D_MODEL ** -0.5),
        'w_out': nrm(13, (DEPTH, D_MODEL, D_MODEL), DN_BETA * D_MODEL ** -0.5),
        'sb_bias': SB_BIAS_INIT + nrm(34, (DEPTH, N_HEADS), 0.5),
        'ln1_g': 1.0 + nrm(14, (DEPTH, D_MODEL), 0.02),
        'ln1_b': nrm(15, (DEPTH, D_MODEL), 0.02),
        'ssm_log_dt': jax.random.uniform(ks[16], (DEPTH, N_SSM_GROUPS), f32, math.log(1e-3), math.log(1e-1)),
        'ssm_lam_re': -0.5 + nrm(17, (DEPTH, N_SSM_GROUPS, SSM_STATE), 0.01),
        'ssm_lam_im': lam_im,
        'ssm_b_re': nrm(18, (DEPTH, N_SSM_GROUPS, SSM_STATE, SSM_GROUP), (2 * SSM_GROUP) ** -0.5),
        'ssm_b_im': nrm(19, (DEPTH, N_SSM_GROUPS, SSM_STATE, SSM_GROUP), (2 * SSM_GROUP) ** -0.5),
        'ssm_c_re': nrm(20, (DEPTH, N_SSM_GROUPS, SSM_GROUP, SSM_STATE), SSM_STATE ** -0.5),
        'ssm_c_im': nrm(22, (DEPTH, N_SSM_GROUPS, SSM_GROUP, SSM_STATE), SSM_STATE ** -0.5),
        'ssm_d': nrm(23, (DEPTH, D_SSM), 1.0),
        'ssm_w_glu': nrm(24, (DEPTH, D_SSM, D_SSM), D_SSM ** -0.5),
        'ssm_b_glu': nrm(25, (DEPTH, D_SSM), 0.02),
        'conv_w': nrm(26, (DEPTH, CONV_WIDTH, D_CONV), CONV_WIDTH ** -0.5),
        'conv_b': nrm(27, (DEPTH, D_CONV), 0.02),
        'conv_ln_g': 1.0 + nrm(28, (DEPTH, D_CONV), 0.02),
        'conv_ln_b': nrm(29, (DEPTH, D_CONV), 0.02),
        'ln2_g': 1.0 + nrm(30, (DEPTH, D_MODEL), 0.02),
        'ln2_b': nrm(31, (DEPTH, D_MODEL), 0.02),
        'w_ff1': nrm(32, (DEPTH, D_MODEL, D_FF), D_MODEL ** -0.5),
        'w_ff2': nrm(33, (DEPTH, D_FF, D_MODEL), DN_BETA * D_FF ** -0.5),
    }


def reference(x_prompt, x_sample, cache_k, cache_v, state_ssm_re, state_ssm_im, state_conv, page_table,
              c_prompt, c_sample, w_ada, b_ada, w_in, w_out, sb_bias, ln1_g, ln1_b, ssm_log_dt, ssm_lam_re,
              ssm_lam_im, ssm_b_re, ssm_b_im, ssm_c_re, ssm_c_im, ssm_d, ssm_w_glu, ssm_b_glu, conv_w,
              conv_b, conv_ln_g, conv_ln_b, ln2_g, ln2_b, w_ff1, w_ff2):
    params = dict(w_ada=w_ada, b_ada=b_ada, w_in=w_in, w_out=w_out, sb_bias=sb_bias, ln1_g=ln1_g, ln1_b=ln1_b,
                  ssm_log_dt=ssm_log_dt, ssm_lam_re=ssm_lam_re, ssm_lam_im=ssm_lam_im,
                  ssm_b_re=ssm_b_re, ssm_b_im=ssm_b_im, ssm_c_re=ssm_c_re, ssm_c_im=ssm_c_im,
                  ssm_d=ssm_d, ssm_w_glu=ssm_w_glu, ssm_b_glu=ssm_b_glu, conv_w=conv_w, conv_b=conv_b,
                  conv_ln_g=conv_ln_g, conv_ln_b=conv_ln_b, ln2_g=ln2_g, ln2_b=ln2_b,
                  w_ff1=w_ff1, w_ff2=w_ff2)
    dec_b = x_sample.shape[0]
    n_pages = page_table.shape[1]
    past_len = n_pages * cache_k.shape[2]
    yp, ys = x_prompt, x_sample
    kp_l, vp_l, srp_l, sip_l, cvp_l = [], [], [], [], []
    ks_l, vs_l, srs_l, sis_l, cvs_l = [], [], [], [], []
    for l in range(DEPTH):
        lp = {name: arr[l] for name, arr in params.items()}
        yp, kp, vp, srp, sip, cvp = trunk_layer(yp, c_prompt, lp, None, None, None, None, None, 0)
        k_past = cache_k[l][page_table].reshape(dec_b, past_len, N_HEADS, HEAD_DIM)
        v_past = cache_v[l][page_table].reshape(dec_b, past_len, N_HEADS, HEAD_DIM)
        ys, ksm, vsm, srs, sis, cvs = trunk_layer(ys, c_sample, lp, k_past, v_past, state_ssm_re[l],
                                                  state_ssm_im[l], state_conv[l], past_len)
        kp_l.append(kp); vp_l.append(vp); srp_l.append(srp); sip_l.append(sip); cvp_l.append(cvp)
        ks_l.append(ksm); vs_l.append(vsm); srs_l.append(srs); sis_l.append(sis); cvs_l.append(cvs)
    return (yp, ys, jnp.stack(kp_l), jnp.stack(vp_l), jnp.stack(srp_l), jnp.stack(sip_l), jnp.stack(cvp_l),
            jnp.stack(ks_l), jnp.stack(vs_l), jnp.stack(srs_l), jnp.stack(sis_l), jnp.stack(cvs_l))
```

```python
import functools
import math

import jax
import jax.numpy as jnp
from jax import lax
from jax.experimental import pallas as pl
from jax.experimental.pallas import tpu as pltpu

F32 = jnp.float32
BF16 = jnp.bfloat16

HEAD_DIM = 64
SSM_GROUP = 16
SSM_STATE = 64
LN_EPS = 1e-5
LANES = 128
VMEM_LIMIT_BYTES = 56 * 1024 * 1024


def _params(*sem):
    return pltpu.CompilerParams(dimension_semantics=sem, vmem_limit_bytes=VMEM_LIMIT_BYTES)


def _tile(n, pref):
    t = min(n, pref)
    while n % t:
        t -= 1
    return t


def _rows(ref):
    v = ref[...]
    return v.reshape(v.shape[-2], v.shape[-1])


def _layer_norm(v, g, b):
    mu = jnp.mean(v, axis=-1, keepdims=True)
    d = v - mu
    var = jnp.mean(d * d, axis=-1, keepdims=True)
    return d * lax.rsqrt(var + LN_EPS) * g + b


def _dot(a, b):
    return jnp.dot(a, b, preferred_element_type=F32)


def _dot_nt(a, b):
    return lax.dot_general(a, b, (((1,), (1,)), ((), ())), preferred_element_type=F32)


def _mods_kernel(c_ref, w_ref, b_ref, o_ref):
    c = c_ref[...]
    s = c * jax.nn.sigmoid(c)
    o_ref[0] = _dot(s.astype(BF16), w_ref[0].astype(BF16)) + b_ref[0]


def _mods(c_all, w_ada, b_ada):
    depth, d, n = w_ada.shape
    r = c_all.shape[0]
    tn = _tile(n, 1024)
    return pl.pallas_call(
        _mods_kernel,
        grid=(depth, n // tn),
        in_specs=[
            pl.BlockSpec((r, d), lambda l, j: (0, 0)),
            pl.BlockSpec((1, d, tn), lambda l, j: (l, 0, j)),
            pl.BlockSpec((1, 1, tn), lambda l, j: (l, 0, j)),
        ],
        out_specs=pl.BlockSpec((1, r, tn), lambda l, j: (l, 0, j)),
        out_shape=jax.ShapeDtypeStruct((depth, r, n), F32),
        compiler_params=_params("parallel", "parallel"),
        name="ada_mods",
    )(c_all, w_ada, b_ada.reshape(depth, 1, n))


def _inproj_kernel(x_ref, sc_ref, sh_ref, w_ref, o_ref, h_scr, *, transpose_out):
    @pl.when(pl.program_id(1) == 0)
    def _():
        h = x_ref[...] * (1.0 + _rows(sc_ref)) + _rows(sh_ref)
        h_scr[...] = h.astype(BF16)

    acc = _dot(h_scr[...], w_ref[0].astype(BF16))
    if transpose_out:
        o_ref[0] = acc.T
    else:
        o_ref[...] = acc.astype(o_ref.dtype)


def _in_proj(x2d, mod, mod_spec, w_in, l, col0, ncols, *, tm, rows_per_batch, out_dtype=F32,
             transpose_out=False):
    m, d = x2d.shape
    tn = _tile(math.gcd(col0, ncols), 512)
    jb = col0 // tn
    nb = rows_per_batch // tm if transpose_out else None
    if transpose_out:
        out_shape = jax.ShapeDtypeStruct((m // rows_per_batch, ncols, rows_per_batch), F32)
        out_spec = pl.BlockSpec((1, tn, tm), lambda i, j: (i // nb, j, i % nb))
    else:
        out_shape = jax.ShapeDtypeStruct((m, ncols), out_dtype)
        out_spec = pl.BlockSpec((tm, tn), lambda i, j: (i, j))
    return pl.pallas_call(
        functools.partial(_inproj_kernel, transpose_out=transpose_out),
        grid=(m // tm, ncols // tn),
        in_specs=[
            pl.BlockSpec((tm, d), lambda i, j: (i, 0)),
            mod_spec(1),
            mod_spec(0),
            pl.BlockSpec((1, d, tn), lambda i, j: (l, 0, jb + j)),
        ],
        out_specs=out_spec,
        out_shape=out_shape,
        scratch_shapes=[pltpu.VMEM((tm, d), BF16)],
        compiler_params=_params("parallel", "arbitrary"),
        name="in_proj",
    )(x2d, mod, mod, w_in)


def _sb_logs(z):
    t = jnp.log1p(jnp.exp(-jnp.abs(z)))
    return jnp.minimum(z, 0.0) - t, jnp.minimum(-z, 0.0) - t


def _tail_sums(lk, u):
    hi = lk.astype(BF16)
    lo = (lk - hi.astype(F32)).astype(BF16)
    r = _dot(hi, u) + _dot(lo, u)
    return r[:, :LANES], r[:, LANES:]


def _tail_matrix():
    j = lax.broadcasted_iota(jnp.int32, (LANES, 2 * LANES), 0)
    s = lax.broadcasted_iota(jnp.int32, (LANES, 2 * LANES), 1)
    return ((j > s) | (s >= LANES)).astype(BF16)


def _attn_kernel(bias_ref, q_ref, k_ref, v_ref, u_ref, o_ref, kb_scr, vb_scr, *, bias0, tq, scale):
    hp = pl.program_id(1)
    qi = pl.program_id(2)

    @pl.when(qi == 0)
    def _():
        kb_scr[...] = k_ref[0].astype(BF16)
        vb_scr[...] = v_ref[0].astype(BF16)

    u = u_ref[...]
    qpos = qi * tq + lax.broadcasted_iota(jnp.int32, (tq, LANES), 0)
    col = lax.broadcasted_iota(jnp.int32, (tq, LANES), 1)
    nkb = (qi + 1) * (tq // LANES)
    for hh in range(LANES // HEAD_DIM):
        rows = slice(hh * HEAD_DIM, (hh + 1) * HEAD_DIM)
        q = q_ref[:, rows]
        bias = bias_ref[bias0 + hp * (LANES // HEAD_DIM) + hh]

        def body(jj, carry, q=q, bias=bias, rows=rows):
            acc, c = carry
            off = pl.multiple_of((nkb - 1 - jj) * LANES, LANES)
            kt = kb_scr[rows, pl.ds(off, LANES)]
            vt = vb_scr[rows, pl.ds(off, LANES)]
            z = _dot(q, kt) * scale + bias
            lb, lk = _sb_logs(z)
            mask = (off + col) < qpos
            tail, total = _tail_sums(jnp.where(mask, lk, 0.0), u)
            w = jnp.where(mask, jnp.exp(lb + tail + c), 0.0)
            return acc + _dot_nt(w.astype(BF16), vt), c + total

        acc, _ = lax.fori_loop(0, nkb, body,
                               (jnp.zeros((tq, HEAD_DIM), F32), jnp.zeros((tq, LANES), F32)))
        o_ref[:, rows] = acc.astype(o_ref.dtype)


def _attn_prompt(q, kvt, bias_flat, u, l, n_heads):
    m, d_att = q.shape
    bsz, _, t = kvt.shape
    tq = _tile(t, 256)
    nq = t // tq
    hp_n = d_att // LANES
    kernel = functools.partial(_attn_kernel, bias0=l * n_heads, tq=tq, scale=HEAD_DIM ** -0.5)
    return pl.pallas_call(
        kernel,
        grid_spec=pltpu.PrefetchScalarGridSpec(
            num_scalar_prefetch=1,
            grid=(bsz, hp_n, nq),
            in_specs=[
                pl.BlockSpec((tq, LANES), lambda b, h, i, s: (b * nq + i, h)),
                pl.BlockSpec((1, LANES, t), lambda b, h, i, s: (b, h, 0)),
                pl.BlockSpec((1, LANES, t), lambda b, h, i, s: (b, hp_n + h, 0)),
                pl.BlockSpec((LANES, 2 * LANES), lambda b, h, i, s: (0, 0)),
            ],
            out_specs=pl.BlockSpec((tq, LANES), lambda b, h, i, s: (b * nq + i, h)),
            scratch_shapes=[pltpu.VMEM((LANES, t), BF16), pltpu.VMEM((LANES, t), BF16)],
        ),
        out_shape=jax.ShapeDtypeStruct((m, d_att), BF16),
        compiler_params=_params("parallel", "parallel", "arbitrary"),
        name="sb_attn_prompt",
    )(bias_flat, q, kvt, kvt, u)


def _decode_kernel(pt_ref, q_ref, bias_ref, u_ref, *refs, pages, scale):
    k_refs = refs[:pages]
    v_refs = refs[pages:2 * pages]
    o_ref, acc_scr, c_scr = refs[2 * pages:]
    s = pl.program_id(1)

    @pl.when(s == 0)
    def _():
        acc_scr[...] = jnp.zeros_like(acc_scr)
        c_scr[...] = jnp.zeros_like(c_scr)

    q = q_ref[0]
    bias = bias_ref[0]
    u = u_ref[...]
    acc = acc_scr[...]
    c = c_scr[...]
    for r in range(pages):
        kt = k_refs[r][0, 0]
        vt = v_refs[r][0, 0]
        z = jnp.sum(kt * q, axis=1) * scale + bias
        lb, lk = _sb_logs(z)
        tail, total = _tail_sums(lk, u)
        w = jnp.exp(lb + tail + c)
        acc = acc + w[:, None, :] * vt
        c = c + total
    acc_scr[...] = acc
    c_scr[...] = c

    @pl.when(s == pl.num_programs(1) - 1)
    def _():
        o_ref[0] = jnp.sum(acc, axis=2)


def _attn_decode(q4, cache_kt, cache_vt, page_table_flat, bias_col, u, l, n_pages):
    bsz, n_heads = q4.shape[:2]
    page = cache_kt.shape[-1]
    assert page == LANES
    pages = _tile(n_pages, 4)
    steps = n_pages // pages

    def page_spec(r):
        def imap(b, s, pt):
            return (l, pt[b * n_pages + (n_pages - 1 - (s * pages + r))], 0, 0, 0)
        return pl.BlockSpec((1, 1, n_heads, HEAD_DIM, page), imap)

    kernel = functools.partial(_decode_kernel, pages=pages, scale=HEAD_DIM ** -0.5)
    return pl.pallas_call(
        kernel,
        grid_spec=pltpu.PrefetchScalarGridSpec(
            num_scalar_prefetch=1,
            grid=(bsz, steps),
            in_specs=[
                pl.BlockSpec((1, n_heads, HEAD_DIM, 1), lambda b, s, pt: (b, 0, 0, 0)),
                pl.BlockSpec((1, n_heads, 1), lambda b, s, pt: (l, 0, 0)),
                pl.BlockSpec((LANES, 2 * LANES), lambda b, s, pt: (0, 0)),
            ] + [page_spec(r) for r in range(pages)] * 2,
            out_specs=pl.BlockSpec((1, n_heads, HEAD_DIM), lambda b, s, pt: (b, 0, 0)),
            scratch_shapes=[pltpu.VMEM((n_heads, HEAD_DIM, page), F32),
                            pltpu.VMEM((n_heads, LANES), F32)],
        ),
        out_shape=jax.ShapeDtypeStruct((bsz, n_heads, HEAD_DIM), F32),
        compiler_params=_params("parallel", "arbitrary"),
        name="sb_attn_decode",
    )(page_table_flat, q4, bias_col, u, *([cache_kt] * pages), *([cache_vt] * pages))


def _ssm_disc_kernel(ldt_ref, lr_ref, li_ref, bre_ref, bim_ref, are_ref, aim_ref, bbre_ref, bbim_ref):
    dt = jnp.exp(ldt_ref[...])
    lr = lr_ref[...]
    li = li_ref[...]
    mag = jnp.exp(lr * dt)
    ang = li * dt
    ab_re = mag * jnp.cos(ang)
    ab_im = mag * jnp.sin(ang)
    den = lr * lr + li * li
    nr = ab_re - 1.0
    f_re = (nr * lr + ab_im * li) / den
    f_im = (ab_im * lr - nr * li) / den
    are_ref[...] = ab_re
    aim_ref[...] = ab_im
    b_re = bre_ref[...]
    b_im = bim_ref[...]
    f_re = f_re[:, :, None, :]
    f_im = f_im[:, :, None, :]
    bbre_ref[...] = f_re * b_re - f_im * b_im
    bbim_ref[...] = f_re * b_im + f_im * b_re


def _ssm_discretise(log_dt, lam_re, lam_im, bt_re, bt_im):
    depth, g, p = lam_re.shape
    return pl.pallas_call(
        _ssm_disc_kernel,
        out_shape=[jax.ShapeDtypeStruct((depth, g, p), F32)] * 2
        + [jax.ShapeDtypeStruct(bt_re.shape, F32)] * 2,
        name="ssm_discretise",
    )(log_dt.reshape(depth, g, 1), lam_re, lam_im, bt_re, bt_im)


def _block_diag(blocks):
    depth, g, r, c = blocks.shape
    eye = jnp.eye(g, dtype=blocks.dtype)
    return (blocks[:, :, :, None, :] * eye[None, :, None, :, None]).reshape(depth, g * r, g * c)


def _ssm_outputs(x_re_im, u2, cd_ref, d_ref, wg_ref, bg_ref):
    y = _dot(x_re_im.astype(BF16), cd_ref[0]) + _rows(d_ref) * u2
    y = jax.nn.gelu(y, approximate=True)
    gate = jax.nn.sigmoid(_dot(y.astype(BF16), wg_ref[0].astype(BF16)) + _rows(bg_ref))
    return y * gate


def _ssm_prompt_kernel(u_ref, bd_ref, cd_ref, are_ref, aim_ref, d_ref, wg_ref, bg_ref,
                       y_ref, sre_ref, sim_ref, bu_scr, st_re, st_im, *, lane_group):
    c = pl.program_id(0)
    bsz, tc, d_ssm = u_ref.shape
    n_state = st_re.shape[1]

    @pl.when(c == 0)
    def _():
        st_re[...] = jnp.zeros_like(st_re)
        st_im[...] = jnp.zeros_like(st_im)

    u2 = u_ref[...].reshape(bsz * tc, d_ssm)
    n_tiles = n_state // LANES
    bu = _dot(u2.astype(BF16), bd_ref[0])
    for k in range(2 * n_tiles):
        bu_scr[k] = bu[:, k * LANES:(k + 1) * LANES]

    for lg in range(n_tiles // lane_group):
        tiles = list(range(lg * lane_group, (lg + 1) * lane_group))
        a_re = [jnp.broadcast_to(are_ref[0, :, k * LANES:(k + 1) * LANES], (bsz, LANES)) for k in tiles]
        a_im = [jnp.broadcast_to(aim_ref[0, :, k * LANES:(k + 1) * LANES], (bsz, LANES)) for k in tiles]

        def step(s, carry, tiles=tiles, a_re=a_re, a_im=a_im):
            x_re, x_im = carry
            rows = pl.ds(s, bsz, stride=tc)
            n_re, n_im = [], []
            for i, k in enumerate(tiles):
                r = a_re[i] * x_re[i] - a_im[i] * x_im[i] + bu_scr[k, rows, :]
                m = a_re[i] * x_im[i] + a_im[i] * x_re[i] + bu_scr[n_tiles + k, rows, :]
                bu_scr[k, rows, :] = r
                bu_scr[n_tiles + k, rows, :] = m
                n_re.append(r)
                n_im.append(m)
            return tuple(n_re), tuple(n_im)

        init = (tuple(st_re[:, k * LANES:(k + 1) * LANES] for k in tiles),
                tuple(st_im[:, k * LANES:(k + 1) * LANES] for k in tiles))
        x_re, x_im = lax.fori_loop(0, tc, step, init)
        for i, k in enumerate(tiles):
            st_re[:, k * LANES:(k + 1) * LANES] = x_re[i]
            st_im[:, k * LANES:(k + 1) * LANES] = x_im[i]

    x = jnp.concatenate([bu_scr[k] for k in range(2 * n_tiles)], axis=1)
    out = _ssm_outputs(x, u2, cd_ref, d_ref, wg_ref, bg_ref)
    y_ref[...] = out.reshape(bsz, tc, d_ssm).astype(y_ref.dtype)

    @pl.when(c == pl.num_programs(0) - 1)
    def _():
        sre_ref[...] = st_re[...]
        sim_ref[...] = st_im[...]


def _ssm_prompt(proj3, u_col, bd, cd, a_re, a_im, ssm_d, w_glu, b_glu, l):
    bsz, t, _ = proj3.shape
    d_ssm = bd.shape[1]
    n_state = a_re.shape[-1]
    tc = _tile(t, 256)
    lane_group = _tile(n_state // LANES, 4)
    lsel = lambda *shape: pl.BlockSpec((1,) + shape, lambda c: (l,) + (0,) * len(shape))
    return pl.pallas_call(
        functools.partial(_ssm_prompt_kernel, lane_group=lane_group),
        grid=(t // tc,),
        in_specs=[
            pl.BlockSpec((bsz, tc, d_ssm), lambda c: (0, c, u_col)),
            lsel(d_ssm, 2 * n_state), lsel(2 * n_state, d_ssm),
            lsel(1, n_state), lsel(1, n_state),
            lsel(1, d_ssm), lsel(d_ssm, d_ssm), lsel(1, d_ssm),
        ],
        out_specs=[
            pl.BlockSpec((bsz, tc, d_ssm), lambda c: (0, c, 0)),
            pl.BlockSpec((bsz, n_state), lambda c: (0, 0)),
            pl.BlockSpec((bsz, n_state), lambda c: (0, 0)),
        ],
        out_shape=[jax.ShapeDtypeStruct((bsz, t, d_ssm), BF16),
                   jax.ShapeDtypeStruct((bsz, n_state), F32),
                   jax.ShapeDtypeStruct((bsz, n_state), F32)],
        scratch_shapes=[pltpu.VMEM((2 * n_state // LANES, bsz * tc, LANES), F32),
                        pltpu.VMEM((bsz, n_state), F32),
                        pltpu.VMEM((bsz, n_state), F32)],
        compiler_params=_params("arbitrary"),
        name="ssm_prompt",
    )(proj3, bd, cd, a_re, a_im, ssm_d, w_glu, b_glu)


def _conv_finish(acc, g_ref, b_ref):
    y = _layer_norm(acc, _rows(g_ref), _rows(b_ref))
    return y * jax.nn.sigmoid(y)


def _conv_prompt_kernel(ca_ref, cb_ref, cw_ref, cbias_ref, g_ref, b_ref, y_ref, buf_ref, gp_scr,
                        *, hist):
    c = pl.program_id(1)
    tc = ca_ref.shape[1]
    width = cw_ref.shape[1]
    pad = hist - (width - 1)

    @pl.when(c == 0)
    def _():
        gp_scr[0:hist, :] = jnp.zeros((hist, gp_scr.shape[1]), F32)

    gp_scr[hist:hist + tc, :] = ca_ref[0] * jax.nn.sigmoid(cb_ref[0])
    acc = _rows(cbias_ref) + gp_scr[pad:pad + tc, :] * cw_ref[0, 0:1, :]
    for w in range(1, width):
        acc = acc + gp_scr[pad + w:pad + w + tc, :] * cw_ref[0, w:w + 1, :]
    y_ref[0] = _conv_finish(acc, g_ref, b_ref).astype(y_ref.dtype)

    @pl.when(c == pl.num_programs(1) - 1)
    def _():
        buf_ref[0] = gp_scr[tc + pad:tc + hist, :]

    gp_scr[0:hist, :] = gp_scr[tc:tc + hist, :]


def _conv_prompt(proj3, ca_col, conv_w, conv_b, ln_g, ln_b, l):
    bsz, t, _ = proj3.shape
    depth, width, d_conv = conv_w.shape
    tc = _tile(t, 256)
    hist = 32
    assert width - 1 <= hist <= tc
    lsel = lambda *shape: pl.BlockSpec((1,) + shape, lambda b, c: (l,) + (0,) * len(shape))
    return pl.pallas_call(
        functools.partial(_conv_prompt_kernel, hist=hist),
        grid=(bsz, t // tc),
        in_specs=[
            pl.BlockSpec((1, tc, d_conv), lambda b, c: (b, c, ca_col)),
            pl.BlockSpec((1, tc, d_conv), lambda b, c: (b, c, ca_col + 1)),
            lsel(width, d_conv), lsel(1, d_conv), lsel(1, d_conv), lsel(1, d_conv),
        ],
        out_specs=[
            pl.BlockSpec((1, tc, d_conv), lambda b, c: (b, c, 0)),
            pl.BlockSpec((1, width - 1, d_conv), lambda b, c: (b, 0, 0)),
        ],
        out_shape=[jax.ShapeDtypeStruct((bsz, t, d_conv), BF16),
                   jax.ShapeDtypeStruct((bsz, width - 1, d_conv), F32)],
        scratch_shapes=[pltpu.VMEM((tc + hist, d_conv), F32)],
        compiler_params=_params("parallel", "arbitrary"),
        name="conv_prompt",
    )(proj3, proj3, conv_w, conv_b, ln_g, ln_b)


def _sample_mix_kernel(u_ref, ca_ref, cb_ref, hre_ref, him_ref, buf_ref,
                       bd_ref, cd_ref, are_ref, aim_ref, d_ref, wg_ref, bg_ref,
                       cw_ref, cbias_ref, g_ref, b_ref,
                       yssm_ref, sre_ref, sim_ref, yconv_ref, nbuf_ref):
    n_state = hre_ref.shape[-1]
    u = u_ref[...]
    bu = _dot(u.astype(BF16), bd_ref[0])
    a_re = are_ref[0]
    a_im = aim_ref[0]
    h_re = hre_ref[0]
    h_im = him_ref[0]
    x_re = bu[:, :n_state] + (a_re * h_re - a_im * h_im)
    x_im = bu[:, n_state:] + (a_re * h_im + a_im * h_re)
    sre_ref[...] = x_re
    sim_ref[...] = x_im
    x = jnp.concatenate([x_re, x_im], axis=1)
    yssm_ref[...] = _ssm_outputs(x, u, cd_ref, d_ref, wg_ref, bg_ref).astype(yssm_ref.dtype)

    width = cw_ref.shape[1]
    g = ca_ref[...] * jax.nn.sigmoid(cb_ref[...])
    acc = _rows(cbias_ref) + g * cw_ref[0, width - 1:width, :]
    for w in range(width - 1):
        acc = acc + buf_ref[0, w] * cw_ref[0, w:w + 1, :]
    yconv_ref[...] = _conv_finish(acc, g_ref, b_ref).astype(yconv_ref.dtype)
    for w in range(width - 2):
        nbuf_ref[w] = buf_ref[0, w + 1]
    nbuf_ref[width - 2] = g


def _sample_mix(proj_s3, h_re, h_im, conv_buf_t, bd, cd, a_re, a_im, ssm_d, w_glu, b_glu,
                conv_w, conv_b, ln_g, ln_b, l):
    bsz = proj_s3.shape[0]
    d_ssm = bd.shape[1]
    n_state = a_re.shape[-1]
    _, width, d_conv = conv_w.shape
    assert d_ssm == d_conv
    lsel = lambda *shape: pl.BlockSpec((1,) + shape, lambda i: (l,) + (0,) * len(shape))
    col = lambda j: pl.BlockSpec((bsz, d_ssm), lambda i: (0, j))
    return pl.pallas_call(
        _sample_mix_kernel,
        grid=(1,),
        in_specs=[
            col(0), col(1), col(2),
            lsel(bsz, n_state), lsel(bsz, n_state), lsel(width - 1, bsz, d_conv),
            lsel(d_ssm, 2 * n_state), lsel(2 * n_state, d_ssm),
            lsel(1, n_state), lsel(1, n_state),
            lsel(1, d_ssm), lsel(d_ssm, d_ssm), lsel(1, d_ssm),
            lsel(width, d_conv), lsel(1, d_conv), lsel(1, d_conv), lsel(1, d_conv),
        ],
        out_specs=[
            pl.BlockSpec((bsz, d_ssm), lambda i: (0, 0)),
            pl.BlockSpec((bsz, n_state), lambda i: (0, 0)),
            pl.BlockSpec((bsz, n_state), lambda i: (0, 0)),
            pl.BlockSpec((bsz, d_conv), lambda i: (0, 0)),
            pl.BlockSpec((width - 1, bsz, d_conv), lambda i: (0, 0, 0)),
        ],
        out_shape=[jax.ShapeDtypeStruct((bsz, d_ssm), BF16),
                   jax.ShapeDtypeStruct((bsz, n_state), F32),
                   jax.ShapeDtypeStruct((bsz, n_state), F32),
                   jax.ShapeDtypeStruct((bsz, d_conv), BF16),
                   jax.ShapeDtypeStruct((width - 1, bsz, d_conv), F32)],
        compiler_params=_params("arbitrary"),
        name="sample_mix",
    )(proj_s3, proj_s3, proj_s3, h_re, h_im, conv_buf_t, bd, cd, a_re, a_im, ssm_d, w_glu, b_glu,
      conv_w, conv_b, ln_g, ln_b)


def _outproj_kernel(att_ref, ssm_ref, conv_ref, x_ref, g1_ref, lng_ref, lnb_ref, w_ref, o_ref,
                    *, alpha, nk_att, nk_ssm):
    kc = pl.program_id(1)
    w = w_ref[0].astype(BF16)

    @pl.when(kc == 0)
    def _():
        o_ref[...] = jnp.zeros_like(o_ref)

    @pl.when(kc < nk_att)
    def _():
        o_ref[...] += _dot(att_ref[...], w)

    @pl.when((kc >= nk_att) & (kc < nk_att + nk_ssm))
    def _():
        o_ref[...] += _dot(ssm_ref[...], w)

    @pl.when(kc >= nk_att + nk_ssm)
    def _():
        o_ref[...] += _dot(conv_ref[...], w)

    @pl.when(kc == pl.num_programs(1) - 1)
    def _():
        v = alpha * x_ref[...] + _rows(g1_ref) * o_ref[...]
        o_ref[...] = _layer_norm(v, _rows(lng_ref), _rows(lnb_ref))


def _out_proj_ln(att, y_ssm, y_conv, x2d, mod, mod_spec, ln_g, ln_b, w_out, l, alpha, *, tm):
    m, d = x2d.shape
    d_att, d_ssm, d_conv = att.shape[1], y_ssm.shape[1], y_conv.shape[1]
    tk = _tile(math.gcd(d_att, d_ssm, d_conv), 512)
    nk_att, nk_ssm, nk_conv = d_att // tk, d_ssm // tk, d_conv // tk
    nk = nk_att + nk_ssm + nk_conv
    lsel = lambda *shape: pl.BlockSpec((1,) + shape, lambda i, k: (l,) + (0,) * len(shape))
    kernel = functools.partial(_outproj_kernel, alpha=alpha, nk_att=nk_att, nk_ssm=nk_ssm)
    return pl.pallas_call(
        kernel,
        grid=(m // tm, nk),
        in_specs=[
            pl.BlockSpec((tm, tk), lambda i, k: (i, jnp.minimum(k, nk_att - 1))),
            pl.BlockSpec((tm, tk), lambda i, k: (i, jnp.clip(k - nk_att, 0, nk_ssm - 1))),
            pl.BlockSpec((tm, tk), lambda i, k: (i, jnp.clip(k - nk_att - nk_ssm, 0, nk_conv - 1))),
            pl.BlockSpec((tm, d), lambda i, k: (i, 0)),
            mod_spec(2),
            lsel(1, d), lsel(1, d),
            pl.BlockSpec((1, tk, d), lambda i, k: (l, k, 0)),
        ],
        out_specs=pl.BlockSpec((tm, d), lambda i, k: (i, 0)),
        out_shape=jax.ShapeDtypeStruct((m, d), F32),
        compiler_params=_params("parallel", "arbitrary"),
        name="out_proj_ln1",
    )(att, y_ssm, y_conv, x2d, mod, ln_g, ln_b, w_out)


def _ffn_kernel(x_ref, sc_ref, sh_ref, g2_ref, lng_ref, lnb_ref, w1_ref, w2_ref, o_ref, h_scr,
                *, alpha):
    f = pl.program_id(1)

    @pl.when(f == 0)
    def _():
        h = x_ref[...] * (1.0 + _rows(sc_ref)) + _rows(sh_ref)
        h_scr[...] = h.astype(BF16)
        o_ref[...] = jnp.zeros_like(o_ref)

    mid = jnp.maximum(_dot(h_scr[...], w1_ref[0].astype(BF16)), 0.0)
    mid = mid * mid
    o_ref[...] += _dot(mid.astype(BF16), w2_ref[0].astype(BF16))

    @pl.when(f == pl.num_programs(1) - 1)
    def _():
        v = alpha * x_ref[...] + _rows(g2_ref) * o_ref[...]
        o_ref[...] = _layer_norm(v, _rows(lng_ref), _rows(lnb_ref))


def _ffn_ln(x2d, mod, mod_spec, ln_g, ln_b, w_ff1, w_ff2, l, alpha, *, tm):
    m, d = x2d.shape
    d_ff = w_ff1.shape[2]
    tf = _tile(d_ff, 256)
    lsel = lambda *shape: pl.BlockSpec((1,) + shape, lambda i, f: (l,) + (0,) * len(shape))
    return pl.pallas_call(
        functools.partial(_ffn_kernel, alpha=alpha),
        grid=(m // tm, d_ff // tf),
        in_specs=[
            pl.BlockSpec((tm, d), lambda i, f: (i, 0)),
            mod_spec(4), mod_spec(3), mod_spec(5),
            lsel(1, d), lsel(1, d),
            pl.BlockSpec((1, d, tf), lambda i, f: (l, 0, f)),
            pl.BlockSpec((1, tf, d), lambda i, f: (l, f, 0)),
        ],
        out_specs=pl.BlockSpec((tm, d), lambda i, f: (i, 0)),
        out_shape=jax.ShapeDtypeStruct((m, d), F32),
        scratch_shapes=[pltpu.VMEM((tm, d), BF16)],
        compiler_params=_params("parallel", "arbitrary"),
        name="ffn_ln2",
    )(x2d, mod, mod, mod, ln_g, ln_b, w_ff1, w_ff2)


def kernel(x_prompt, x_sample, cache_k, cache_v, state_ssm_re, state_ssm_im, state_conv, page_table,
           c_prompt, c_sample, w_ada, b_ada, w_in, w_out, sb_bias, ln1_g, ln1_b, ssm_log_dt,
           ssm_lam_re, ssm_lam_im, ssm_b_re, ssm_b_im, ssm_c_re, ssm_c_im, ssm_d, ssm_w_glu,
           ssm_b_glu, conv_w, conv_b, conv_ln_g, conv_ln_b, ln2_g, ln2_b, w_ff1, w_ff2):
    bsz, t, d = x_prompt.shape
    dec_b = x_sample.shape[0]
    depth = w_in.shape[0]
    n_heads = cache_k.shape[3]
    d_att = n_heads * HEAD_DIM
    d_ssm = ssm_d.shape[1]
    d_conv = conv_b.shape[1]
    n_groups, n_state_g = ssm_lam_re.shape[1:]
    n_state = n_groups * n_state_g
    n_pages = page_table.shape[1]
    alpha = (2 * depth) ** 0.25
    assert d_ssm == d_conv and d_att % LANES == 0
    m = bsz * t

    n_rows = -(-(bsz + dec_b) // 8) * 8
    c_all = jnp.concatenate([c_prompt, c_sample, jnp.zeros((n_rows - bsz - dec_b, d), F32)], axis=0)
    mods = _mods(c_all, w_ada, b_ada)
    mod_p = mods[:, :bsz].reshape(depth, bsz, 6, 1, d).transpose(0, 2, 1, 3, 4)
    mod_s = mods[:, bsz:bsz + dec_b].reshape(depth, dec_b, 6, d).transpose(0, 2, 1, 3)

    tm = _tile(t, 512)
    blocks_per_seq = t // tm

    def vec(p):
        return p.reshape(depth, 1, p.shape[-1])

    bt_re = ssm_b_re.transpose(0, 1, 3, 2)
    bt_im = ssm_b_im.transpose(0, 1, 3, 2)
    ab_re, ab_im, bbt_re, bbt_im = _ssm_discretise(ssm_log_dt, ssm_lam_re, ssm_lam_im, bt_re, bt_im)
    bd = jnp.concatenate([_block_diag(bbt_re), _block_diag(bbt_im)], axis=2).astype(BF16)
    cd = jnp.concatenate([_block_diag(ssm_c_re.transpose(0, 1, 3, 2)),
                          _block_diag(-ssm_c_im.transpose(0, 1, 3, 2))], axis=1).astype(BF16)
    a_re = ab_re.reshape(depth, 1, n_state)
    a_im = ab_im.reshape(depth, 1, n_state)
    ssm_d3, b_glu3 = vec(ssm_d), vec(ssm_b_glu)
    conv_b3, cln_g3, cln_b3 = vec(conv_b), vec(conv_ln_g), vec(conv_ln_b)
    ln1_g3, ln1_b3, ln2_g3, ln2_b3 = vec(ln1_g), vec(ln1_b), vec(ln2_g), vec(ln2_b)

    cache_kt = cache_k.transpose(0, 1, 3, 4, 2)
    cache_vt = cache_v.transpose(0, 1, 3, 4, 2)
    pt_flat = page_table.reshape(-1).astype(jnp.int32)
    bias_flat = sb_bias.reshape(-1)
    bias_col = sb_bias.reshape(depth, n_heads, 1)
    h_re_all = state_ssm_re.reshape(depth, dec_b, n_state)
    h_im_all = state_ssm_im.reshape(depth, dec_b, n_state)
    conv_buf_t = state_conv.transpose(0, 2, 1, 3)
    u_tail = _tail_matrix()

    xp = x_prompt.reshape(m, d)
    xs = x_sample.reshape(dec_b, d)
    rest = d_ssm + 2 * d_conv
    kp_l, vp_l, srp_l, sip_l, cvp_l = [], [], [], [], []
    ks_l, vs_l, srs_l, sis_l, cvs_l = [], [], [], [], []
    for l in range(depth):
        def spec_p(k, l=l):
            return pl.BlockSpec((1, 1, 1, 1, d), lambda i, j: (l, k, i // blocks_per_seq, 0, 0))

        def spec_s(k, l=l):
            return pl.BlockSpec((1, 1, dec_b, d), lambda i, j: (l, k, 0, 0))

        q = _in_proj(xp, mod_p, spec_p, w_in, l, 0, d_att, tm=tm, rows_per_batch=t, out_dtype=BF16)
        kvt = _in_proj(xp, mod_p, spec_p, w_in, l, d_att, 2 * d_att, tm=tm, rows_per_batch=t,
                       transpose_out=True)
        proj3 = _in_proj(xp, mod_p, spec_p, w_in, l, 3 * d_att, rest, tm=tm, rows_per_batch=t)
        proj3 = proj3.reshape(bsz, t, rest)
        att = _attn_prompt(q, kvt, bias_flat, u_tail, l, n_heads)
        y_ssm, s_re, s_im = _ssm_prompt(proj3, 0, bd, cd, a_re, a_im, ssm_d3, ssm_w_glu, b_glu3, l)
        y_conv, cbuf = _conv_prompt(proj3, d_ssm // d_conv, conv_w, conv_b3, cln_g3, cln_b3, l)
        x1 = _out_proj_ln(att, y_ssm.reshape(m, d_ssm), y_conv.reshape(m, d_conv), xp, mod_p, spec_p,
                          ln1_g3, ln1_b3, w_out, l, alpha, tm=tm)
        xp = _ffn_ln(x1, mod_p, spec_p, ln2_g3, ln2_b3, w_ff1, w_ff2, l, alpha, tm=tm)
        kp_l.append(kvt[:, :d_att]); vp_l.append(kvt[:, d_att:])
        srp_l.append(s_re); sip_l.append(s_im); cvp_l.append(cbuf)

        proj_s = _in_proj(xs, mod_s, spec_s, w_in, l, 0, 3 * d_att + rest, tm=dec_b,
                          rows_per_batch=1)
        q_s = proj_s[:, :d_att].reshape(dec_b, n_heads, HEAD_DIM, 1)
        att_s = _attn_decode(q_s, cache_kt, cache_vt, pt_flat, bias_col, u_tail, l, n_pages)
        ys_ssm, ss_re, ss_im, ys_conv, nbuf = _sample_mix(
            proj_s[:, 3 * d_att:], h_re_all, h_im_all, conv_buf_t, bd, cd, a_re, a_im, ssm_d3,
            ssm_w_glu, b_glu3, conv_w, conv_b3, cln_g3, cln_b3, l)
        xs1 = _out_proj_ln(att_s.reshape(dec_b, d_att).astype(BF16), ys_ssm, ys_conv, xs, mod_s,
                           spec_s, ln1_g3, ln1_b3, w_out, l, alpha, tm=dec_b)
        xs = _ffn_ln(xs1, mod_s, spec_s, ln2_g3, ln2_b3, w_ff1, w_ff2, l, alpha, tm=dec_b)
        ks_l.append(proj_s[:, d_att:2 * d_att]); vs_l.append(proj_s[:, 2 * d_att:3 * d_att])
        srs_l.append(ss_re); sis_l.append(ss_im); cvs_l.append(nbuf)

    def heads_t(xs_):
        return jnp.stack(xs_).reshape(depth, bsz, n_heads, HEAD_DIM, t).transpose(0, 1, 4, 2, 3)

    def state(xs_, b):
        return jnp.stack(xs_).reshape(depth, b, n_groups, n_state_g)

    return (xp.reshape(bsz, t, d), xs.reshape(dec_b, 1, d),
            heads_t(kp_l), heads_t(vp_l), state(srp_l, bsz), state(sip_l, bsz), jnp.stack(cvp_l),
            jnp.stack(ks_l).reshape(depth, dec_b, 1, n_heads, HEAD_DIM),
            jnp.stack(vs_l).reshape(depth, dec_b, 1, n_heads, HEAD_DIM),
            state(srs_l, dec_b), state(sis_l, dec_b),
            jnp.stack(cvs_l).transpose(0, 2, 1, 3))
```

```python
import functools
import math

import jax
import jax.numpy as jnp
from jax import lax
from jax.experimental import pallas as pl
from jax.experimental.pallas import tpu as pltpu

F32 = jnp.float32
BF16 = jnp.bfloat16

HEAD_DIM = 64
SSM_GROUP = 16
SSM_STATE = 64
LN_EPS = 1e-5
LANES = 128
VMEM_LIMIT_BYTES = 56 * 1024 * 1024


def _params(*sem):
    return pltpu.CompilerParams(dimension_semantics=sem, vmem_limit_bytes=VMEM_LIMIT_BYTES)


def _tile(n, pref):
    t = min(n, pref)
    while n % t:
        t -= 1
    return t


def _rows(ref):
    v = ref[...]
    return v.reshape(v.shape[-2], v.shape[-1])


def _layer_norm(v, g, b):
    mu = jnp.mean(v, axis=-1, keepdims=True)
    d = v - mu
    var = jnp.mean(d * d, axis=-1, keepdims=True)
    return d * lax.rsqrt(var + LN_EPS) * g + b


def _dot(a, b):
    return jnp.dot(a, b, preferred_element_type=F32)


def _dot_nt(a, b):
    return lax.dot_general(a, b, (((1,), (1,)), ((), ())), preferred_element_type=F32)


def _mods_kernel(c_ref, w_ref, b_ref, o_ref):
    c = c_ref[...]
    s = c * jax.nn.sigmoid(c)
    o_ref[0] = _dot(s.astype(BF16), w_ref[0].astype(BF16)) + b_ref[0]


def _mods(c_all, w_ada, b_ada):
    depth, d, n = w_ada.shape
    r = c_all.shape[0]
    tn = _tile(n, 1024)
    return pl.pallas_call(
        _mods_kernel,
        grid=(depth, n // tn),
        in_specs=[
            pl.BlockSpec((r, d), lambda l, j: (0, 0)),
            pl.BlockSpec((1, d, tn), lambda l, j: (l, 0, j)),
            pl.BlockSpec((1, 1, tn), lambda l, j: (l, 0, j)),
        ],
        out_specs=pl.BlockSpec((1, r, tn), lambda l, j: (l, 0, j)),
        out_shape=jax.ShapeDtypeStruct((depth, r, n), F32),
        compiler_params=_params("parallel", "parallel"),
        name="ada_mods",
    )(c_all, w_ada, b_ada.reshape(depth, 1, n))


def _inproj_kernel(x_ref, sc_ref, sh_ref, w_ref, o_ref, h_scr, *, transpose_out):
    @pl.when(pl.program_id(1) == 0)
    def _():
        h = x_ref[...] * (1.0 + _rows(sc_ref)) + _rows(sh_ref)
        h_scr[...] = h.astype(BF16)

    acc = _dot(h_scr[...], w_ref[0].astype(BF16))
    if transpose_out:
        o_ref[0] = acc.T
    else:
        o_ref[...] = acc.astype(o_ref.dtype)


def _in_proj(x2d, mod, mod_spec, w_in, l, col0, ncols, *, tm, rows_per_batch, out_dtype=F32,
             transpose_out=False):
    m, d = x2d.shape
    tn = _tile(math.gcd(col0, ncols), 512)
    jb = col0 // tn
    nb = rows_per_batch // tm if transpose_out else None
    if transpose_out:
        out_shape = jax.ShapeDtypeStruct((m // rows_per_batch, ncols, rows_per_batch), F32)
        out_spec = pl.BlockSpec((1, tn, tm), lambda i, j: (i // nb, j, i % nb))
    else:
        out_shape = jax.ShapeDtypeStruct((m, ncols), out_dtype)
        out_spec = pl.BlockSpec((tm, tn), lambda i, j: (i, j))
    return pl.pallas_call(
        functools.partial(_inproj_kernel, transpose_out=transpose_out),
        grid=(m // tm, ncols // tn),
        in_specs=[
            pl.BlockSpec((tm, d), lambda i, j: (i, 0)),
            mod_spec(1),
            mod_spec(0),
            pl.BlockSpec((1, d, tn), lambda i, j: (l, 0, jb + j)),
        ],
        out_specs=out_spec,
        out_shape=out_shape,
        scratch_shapes=[pltpu.VMEM((tm, d), BF16)],
        compiler_params=_params("parallel", "arbitrary"),
        name="in_proj",
    )(x2d, mod, mod, w_in)


def _sb_logs(z):
    t = jnp.log(1.0 + jnp.exp(-jnp.abs(z)))
    return jnp.minimum(z, 0.0) - t, jnp.maximum(z, 0.0) + t


def _tail_sums(x, u):
    hi = x.astype(BF16)
    lo = (x - hi.astype(F32)).astype(BF16)
    r = _dot(jnp.concatenate([hi, lo], axis=1), u)
    return r[:, :LANES], r[:, LANES:]


def _tail_matrix():
    j = lax.broadcasted_iota(jnp.int32, (2 * LANES, 2 * LANES), 0) % LANES
    s = lax.broadcasted_iota(jnp.int32, (2 * LANES, 2 * LANES), 1)
    return ((j > s) | (s >= LANES)).astype(BF16)


def _attn_kernel(bias_ref, q_ref, k_ref, v_ref, u_ref, o_ref, kb_scr, vb_scr, *, bias0, scale):
    hp = pl.program_id(1)
    qi = pl.program_id(2)
    tq = q_ref.shape[0]
    heads = LANES // HEAD_DIM

    @pl.when(qi == 0)
    def _():
        kb_scr[...] = k_ref[0].astype(BF16)
        vb_scr[...] = v_ref[0].astype(BF16)

    u = u_ref[...]
    rows = [slice(h * HEAD_DIM, (h + 1) * HEAD_DIM) for h in range(heads)]
    qs = [(q_ref[:, r].astype(F32) * scale).astype(BF16) for r in rows]
    biases = [bias_ref[bias0 + hp * heads + h] for h in range(heads)]

    nb = tq // LANES

    def span(ks, cs, mask):
        off = pl.multiple_of(ks * tq, tq)
        lbs, sps = [], []
        for h in range(heads):
            lb, sp = _sb_logs(_dot(qs[h], kb_scr[rows[h], pl.ds(off, tq)]) + biases[h])
            lbs.append(lb)
            sps.append(sp if mask is None else jnp.where(mask, sp, 0.0))
        lane_blk = lambda x, k: x[:, k * LANES:(k + 1) * LANES]
        stacked = jnp.concatenate([lane_blk(sps[h], k) for h in range(heads) for k in range(nb)], axis=0)
        tail, total = _tail_sums(stacked, u)
        out = []
        for h in range(heads):
            c = cs[h]
            ws = [None] * nb
            for k in reversed(range(nb)):
                r0 = (h * nb + k) * tq
                ws[k] = jnp.exp(lane_blk(lbs[h], k) - tail[r0:r0 + tq] - c)
                c = c + total[r0:r0 + tq]
            w = jnp.concatenate(ws, axis=1)
            if mask is not None:
                w = jnp.where(mask, w, 0.0)
            out.append((_dot_nt(w.astype(BF16), vb_scr[rows[h], pl.ds(off, tq)]), c))
        return out

    causal = (lax.broadcasted_iota(jnp.int32, (tq, tq), 1)
              < lax.broadcasted_iota(jnp.int32, (tq, tq), 0))
    zero_c = jnp.zeros((tq, LANES), F32)
    state = tuple(x for pair in span(qi, [zero_c] * heads, causal) for x in pair)

    def body(p, carry):
        res = span(qi - 1 - p, [carry[2 * h + 1] for h in range(heads)], None)
        return tuple(x for h in range(heads) for x in (carry[2 * h] + res[h][0], res[h][1]))

    state = lax.fori_loop(0, qi, body, state)
    for h in range(heads):
        o_ref[:, rows[h]] = state[2 * h].astype(o_ref.dtype)


def _attn_prompt(q, kvt, bias_flat, u, l, n_heads):
    m, d_att = q.shape
    bsz, _, t = kvt.shape
    tq = _tile(t, 512)
    assert tq % LANES == 0
    nq = t // tq
    hp_n = d_att // LANES
    kernel = functools.partial(_attn_kernel, bias0=l * n_heads, scale=HEAD_DIM ** -0.5)
    return pl.pallas_call(
        kernel,
        grid_spec=pltpu.PrefetchScalarGridSpec(
            num_scalar_prefetch=1,
            grid=(bsz, hp_n, nq),
            in_specs=[
                pl.BlockSpec((tq, LANES), lambda b, h, i, s: (b * nq + i, h)),
                pl.BlockSpec((1, LANES, t), lambda b, h, i, s: (b, h, 0)),
                pl.BlockSpec((1, LANES, t), lambda b, h, i, s: (b, hp_n + h, 0)),
                pl.BlockSpec((2 * LANES, 2 * LANES), lambda b, h, i, s: (0, 0)),
            ],
            out_specs=pl.BlockSpec((tq, LANES), lambda b, h, i, s: (b * nq + i, h)),
            scratch_shapes=[pltpu.VMEM((LANES, t), BF16), pltpu.VMEM((LANES, t), BF16)],
        ),
        out_shape=jax.ShapeDtypeStruct((m, d_att), BF16),
        compiler_params=_params("parallel", "parallel", "arbitrary"),
        name="sb_attn_prompt",
    )(bias_flat, q, kvt, kvt, u)


def _decode_kernel(pt_ref, q_ref, bias_ref, u_ref, *refs, pages, scale):
    k_refs = refs[:pages]
    v_refs = refs[pages:2 * pages]
    o_ref, acc_scr, c_scr = refs[2 * pages:]
    s = pl.program_id(1)

    @pl.when(s == 0)
    def _():
        acc_scr[...] = jnp.zeros_like(acc_scr)
        c_scr[...] = jnp.zeros_like(c_scr)

    q = q_ref[0]
    bias = bias_ref[0]
    u = u_ref[...]
    acc = acc_scr[...]
    c = c_scr[...]
    for r in range(pages):
        kt = k_refs[r][0, 0]
        vt = v_refs[r][0, 0]
        z = jnp.sum(kt * q, axis=1) * scale + bias
        lb, sp = _sb_logs(z)
        tail, total = _tail_sums(sp, u)
        w = jnp.exp(lb - tail - c)
        acc = acc + w[:, None, :] * vt
        c = c + total
    acc_scr[...] = acc
    c_scr[...] = c

    @pl.when(s == pl.num_programs(1) - 1)
    def _():
        o_ref[0] = jnp.sum(acc, axis=2)


def _attn_decode(q4, cache_kt, cache_vt, page_table_flat, bias_col, u, l, n_pages):
    bsz, n_heads = q4.shape[:2]
    page = cache_kt.shape[-1]
    assert page == LANES
    pages = _tile(n_pages, 4)
    steps = n_pages // pages

    def page_spec(r):
        def imap(b, s, pt):
            return (l, pt[b * n_pages + (n_pages - 1 - (s * pages + r))], 0, 0, 0)
        return pl.BlockSpec((1, 1, n_heads, HEAD_DIM, page), imap)

    kernel = functools.partial(_decode_kernel, pages=pages, scale=HEAD_DIM ** -0.5)
    return pl.pallas_call(
        kernel,
        grid_spec=pltpu.PrefetchScalarGridSpec(
            num_scalar_prefetch=1,
            grid=(bsz, steps),
            in_specs=[
                pl.BlockSpec((1, n_heads, HEAD_DIM, 1), lambda b, s, pt: (b, 0, 0, 0)),
                pl.BlockSpec((1, n_heads, 1), lambda b, s, pt: (l, 0, 0)),
                pl.BlockSpec((2 * LANES, 2 * LANES), lambda b, s, pt: (0, 0)),
            ] + [page_spec(r) for r in range(pages)] * 2,
            out_specs=pl.BlockSpec((1, n_heads, HEAD_DIM), lambda b, s, pt: (b, 0, 0)),
            scratch_shapes=[pltpu.VMEM((n_heads, HEAD_DIM, page), F32),
                            pltpu.VMEM((n_heads, LANES), F32)],
        ),
        out_shape=jax.ShapeDtypeStruct((bsz, n_heads, HEAD_DIM), F32),
        compiler_params=_params("parallel", "arbitrary"),
        name="sb_attn_decode",
    )(page_table_flat, q4, bias_col, u, *([cache_kt] * pages), *([cache_vt] * pages))


def _ssm_disc_kernel(ldt_ref, lr_ref, li_ref, bre_ref, bim_ref, are_ref, aim_ref, bbre_ref, bbim_ref):
    dt = jnp.exp(ldt_ref[...])
    lr = lr_ref[...]
    li = li_ref[...]
    mag = jnp.exp(lr * dt)
    ang = li * dt
    ab_re = mag * jnp.cos(ang)
    ab_im = mag * jnp.sin(ang)
    den = lr * lr + li * li
    nr = ab_re - 1.0
    f_re = (nr * lr + ab_im * li) / den
    f_im = (ab_im * lr - nr * li) / den
    are_ref[...] = ab_re
    aim_ref[...] = ab_im
    b_re = bre_ref[...]
    b_im = bim_ref[...]
    f_re = f_re[:, :, None, :]
    f_im = f_im[:, :, None, :]
    bbre_ref[...] = f_re * b_re - f_im * b_im
    bbim_ref[...] = f_re * b_im + f_im * b_re


def _ssm_discretise(log_dt, lam_re, lam_im, bt_re, bt_im):
    depth, g, p = lam_re.shape
    return pl.pallas_call(
        _ssm_disc_kernel,
        out_shape=[jax.ShapeDtypeStruct((depth, g, p), F32)] * 2
        + [jax.ShapeDtypeStruct(bt_re.shape, F32)] * 2,
        name="ssm_discretise",
    )(log_dt.reshape(depth, g, 1), lam_re, lam_im, bt_re, bt_im)


def _block_diag(blocks):
    depth, g, r, c = blocks.shape
    eye = jnp.eye(g, dtype=blocks.dtype)
    return (blocks[:, :, :, None, :] * eye[None, :, None, :, None]).reshape(depth, g * r, g * c)


def _ssm_outputs(x_re_im, u2, cd_ref, d_ref, wg_ref, bg_ref):
    y = _dot(x_re_im.astype(BF16), cd_ref[0]) + _rows(d_ref) * u2
    y = jax.nn.gelu(y, approximate=True)
    gate = jax.nn.sigmoid(_dot(y.astype(BF16), wg_ref[0].astype(BF16)) + _rows(bg_ref))
    return y * gate


def _ssm_prompt_kernel(u_ref, bd_ref, cd_ref, are_ref, aim_ref, d_ref, wg_ref, bg_ref,
                       y_ref, sre_ref, sim_ref, bu_scr, st_re, st_im, *, lane_group):
    c = pl.program_id(0)
    bsz, tc, d_ssm = u_ref.shape
    n_state = st_re.shape[1]

    @pl.when(c == 0)
    def _():
        st_re[...] = jnp.zeros_like(st_re)
        st_im[...] = jnp.zeros_like(st_im)

    u2 = u_ref[...].reshape(bsz * tc, d_ssm)
    n_tiles = n_state // LANES
    bu = _dot(u2.astype(BF16), bd_ref[0])
    for k in range(2 * n_tiles):
        bu_scr[k] = bu[:, k * LANES:(k + 1) * LANES]

    for lg in range(n_tiles // lane_group):
        tiles = list(range(lg * lane_group, (lg + 1) * lane_group))
        a_re = [jnp.broadcast_to(are_ref[0, :, k * LANES:(k + 1) * LANES], (bsz, LANES)) for k in tiles]
        a_im = [jnp.broadcast_to(aim_ref[0, :, k * LANES:(k + 1) * LANES], (bsz, LANES)) for k in tiles]

        def step(s, carry, tiles=tiles, a_re=a_re, a_im=a_im):
            x_re, x_im = carry
            rows = pl.ds(s, bsz, stride=tc)
            n_re, n_im = [], []
            for i, k in enumerate(tiles):
                r = a_re[i] * x_re[i] - a_im[i] * x_im[i] + bu_scr[k, rows, :]
                m = a_re[i] * x_im[i] + a_im[i] * x_re[i] + bu_scr[n_tiles + k, rows, :]
                bu_scr[k, rows, :] = r
                bu_scr[n_tiles + k, rows, :] = m
                n_re.append(r)
                n_im.append(m)
            return tuple(n_re), tuple(n_im)

        init = (tuple(st_re[:, k * LANES:(k + 1) * LANES] for k in tiles),
                tuple(st_im[:, k * LANES:(k + 1) * LANES] for k in tiles))
        x_re, x_im = lax.fori_loop(0, tc, step, init)
        for i, k in enumerate(tiles):
            st_re[:, k * LANES:(k + 1) * LANES] = x_re[i]
            st_im[:, k * LANES:(k + 1) * LANES] = x_im[i]

    x = jnp.concatenate([bu_scr[k] for k in range(2 * n_tiles)], axis=1)
    out = _ssm_outputs(x, u2, cd_ref, d_ref, wg_ref, bg_ref)
    y_ref[...] = out.reshape(bsz, tc, d_ssm).astype(y_ref.dtype)

    @pl.when(c == pl.num_programs(0) - 1)
    def _():
        sre_ref[...] = st_re[...]
        sim_ref[...] = st_im[...]


def _ssm_prompt(proj3, u_col, bd, cd, a_re, a_im, ssm_d, w_glu, b_glu, l):
    bsz, t, _ = proj3.shape
    d_ssm = bd.shape[1]
    n_state = a_re.shape[-1]
    tc = _tile(t, 256)
    lane_group = _tile(n_state // LANES, 4)
    lsel = lambda *shape: pl.BlockSpec((1,) + shape, lambda c: (l,) + (0,) * len(shape))
    return pl.pallas_call(
        functools.partial(_ssm_prompt_kernel, lane_group=lane_group),
        grid=(t // tc,),
        in_specs=[
            pl.BlockSpec((bsz, tc, d_ssm), lambda c: (0, c, u_col)),
            lsel(d_ssm, 2 * n_state), lsel(2 * n_state, d_ssm),
            lsel(1, n_state), lsel(1, n_state),
            lsel(1, d_ssm), lsel(d_ssm, d_ssm), lsel(1, d_ssm),
        ],
        out_specs=[
            pl.BlockSpec((bsz, tc, d_ssm), lambda c: (0, c, 0)),
            pl.BlockSpec((bsz, n_state), lambda c: (0, 0)),
            pl.BlockSpec((bsz, n_state), lambda c: (0, 0)),
        ],
        out_shape=[jax.ShapeDtypeStruct((bsz, t, d_ssm), BF16),
                   jax.ShapeDtypeStruct((bsz, n_state), F32),
                   jax.ShapeDtypeStruct((bsz, n_state), F32)],
        scratch_shapes=[pltpu.VMEM((2 * n_state // LANES, bsz * tc, LANES), F32),
                        pltpu.VMEM((bsz, n_state), F32),
                        pltpu.VMEM((bsz, n_state), F32)],
        compiler_params=_params("arbitrary"),
        name="ssm_prompt",
    )(proj3, bd, cd, a_re, a_im, ssm_d, w_glu, b_glu)


def _conv_finish(acc, g_ref, b_ref):
    y = _layer_norm(acc, _rows(g_ref), _rows(b_ref))
    return y * jax.nn.sigmoid(y)


def _conv_prompt_kernel(ca_ref, cb_ref, cw_ref, cbias_ref, g_ref, b_ref, y_ref, buf_ref, gp_scr,
                        *, hist):
    c = pl.program_id(1)
    tc = ca_ref.shape[1]
    width = cw_ref.shape[1]
    pad = hist - (width - 1)

    @pl.when(c == 0)
    def _():
        gp_scr[0:hist, :] = jnp.zeros((hist, gp_scr.shape[1]), F32)

    gp_scr[hist:hist + tc, :] = ca_ref[0] * jax.nn.sigmoid(cb_ref[0])
    acc = _rows(cbias_ref) + gp_scr[pad:pad + tc, :] * cw_ref[0, 0:1, :]
    for w in range(1, width):
        acc = acc + gp_scr[pad + w:pad + w + tc, :] * cw_ref[0, w:w + 1, :]
    y_ref[0] = _conv_finish(acc, g_ref, b_ref).astype(y_ref.dtype)

    @pl.when(c == pl.num_programs(1) - 1)
    def _():
        buf_ref[0] = gp_scr[tc + pad:tc + hist, :]

    gp_scr[0:hist, :] = gp_scr[tc:tc + hist, :]


def _conv_prompt(proj3, ca_col, conv_w, conv_b, ln_g, ln_b, l):
    bsz, t, _ = proj3.shape
    depth, width, d_conv = conv_w.shape
    tc = _tile(t, 256)
    hist = 32
    assert width - 1 <= hist <= tc
    lsel = lambda *shape: pl.BlockSpec((1,) + shape, lambda b, c: (l,) + (0,) * len(shape))
    return pl.pallas_call(
        functools.partial(_conv_prompt_kernel, hist=hist),
        grid=(bsz, t // tc),
        in_specs=[
            pl.BlockSpec((1, tc, d_conv), lambda b, c: (b, c, ca_col)),
            pl.BlockSpec((1, tc, d_conv), lambda b, c: (b, c, ca_col + 1)),
            lsel(width, d_conv), lsel(1, d_conv), lsel(1, d_conv), lsel(1, d_conv),
        ],
        out_specs=[
            pl.BlockSpec((1, tc, d_conv), lambda b, c: (b, c, 0)),
            pl.BlockSpec((1, width - 1, d_conv), lambda b, c: (b, 0, 0)),
        ],
        out_shape=[jax.ShapeDtypeStruct((bsz, t, d_conv), BF16),
                   jax.ShapeDtypeStruct((bsz, width - 1, d_conv), F32)],
        scratch_shapes=[pltpu.VMEM((tc + hist, d_conv), F32)],
        compiler_params=_params("parallel", "arbitrary"),
        name="conv_prompt",
    )(proj3, proj3, conv_w, conv_b, ln_g, ln_b)


def _sample_mix_kernel(u_ref, ca_ref, cb_ref, hre_ref, him_ref, buf_ref,
                       bd_ref, cd_ref, are_ref, aim_ref, d_ref, wg_ref, bg_ref,
                       cw_ref, cbias_ref, g_ref, b_ref,
                       yssm_ref, sre_ref, sim_ref, yconv_ref, nbuf_ref):
    n_state = hre_ref.shape[-1]
    u = u_ref[...]
    bu = _dot(u.astype(BF16), bd_ref[0])
    a_re = are_ref[0]
    a_im = aim_ref[0]
    h_re = hre_ref[0]
    h_im = him_ref[0]
    x_re = bu[:, :n_state] + (a_re * h_re - a_im * h_im)
    x_im = bu[:, n_state:] + (a_re * h_im + a_im * h_re)
    sre_ref[...] = x_re
    sim_ref[...] = x_im
    x = jnp.concatenate([x_re, x_im], axis=1)
    yssm_ref[...] = _ssm_outputs(x, u, cd_ref, d_ref, wg_ref, bg_ref).astype(yssm_ref.dtype)

    width = cw_ref.shape[1]
    g = ca_ref[...] * jax.nn.sigmoid(cb_ref[...])
    acc = _rows(cbias_ref) + g * cw_ref[0, width - 1:width, :]
    for w in range(width - 1):
        acc = acc + buf_ref[0, w] * cw_ref[0, w:w + 1, :]
    yconv_ref[...] = _conv_finish(acc, g_ref, b_ref).astype(yconv_ref.dtype)
    for w in range(width - 2):
        nbuf_ref[w] = buf_ref[0, w + 1]
    nbuf_ref[width - 2] = g


def _sample_mix(proj_s3, h_re, h_im, conv_buf_t, bd, cd, a_re, a_im, ssm_d, w_glu, b_glu,
                conv_w, conv_b, ln_g, ln_b, l):
    bsz = proj_s3.shape[0]
    d_ssm = bd.shape[1]
    n_state = a_re.shape[-1]
    _, width, d_conv = conv_w.shape
    assert d_ssm == d_conv
    lsel = lambda *shape: pl.BlockSpec((1,) + shape, lambda i: (l,) + (0,) * len(shape))
    col = lambda j: pl.BlockSpec((bsz, d_ssm), lambda i: (0, j))
    return pl.pallas_call(
        _sample_mix_kernel,
        grid=(1,),
        in_specs=[
            col(0), col(1), col(2),
            lsel(bsz, n_state), lsel(bsz, n_state), lsel(width - 1, bsz, d_conv),
            lsel(d_ssm, 2 * n_state), lsel(2 * n_state, d_ssm),
            lsel(1, n_state), lsel(1, n_state),
            lsel(1, d_ssm), lsel(d_ssm, d_ssm), lsel(1, d_ssm),
            lsel(width, d_conv), lsel(1, d_conv), lsel(1, d_conv), lsel(1, d_conv),
        ],
        out_specs=[
            pl.BlockSpec((bsz, d_ssm), lambda i: (0, 0)),
            pl.BlockSpec((bsz, n_state), lambda i: (0, 0)),
            pl.BlockSpec((bsz, n_state), lambda i: (0, 0)),
            pl.BlockSpec((bsz, d_conv), lambda i: (0, 0)),
            pl.BlockSpec((width - 1, bsz, d_conv), lambda i: (0, 0, 0)),
        ],
        out_shape=[jax.ShapeDtypeStruct((bsz, d_ssm), BF16),
                   jax.ShapeDtypeStruct((bsz, n_state), F32),
                   jax.ShapeDtypeStruct((bsz, n_state), F32),
                   jax.ShapeDtypeStruct((bsz, d_conv), BF16),
                   jax.ShapeDtypeStruct((width - 1, bsz, d_conv), F32)],
        compiler_params=_params("arbitrary"),
        name="sample_mix",
    )(proj_s3, proj_s3, proj_s3, h_re, h_im, conv_buf_t, bd, cd, a_re, a_im, ssm_d, w_glu, b_glu,
      conv_w, conv_b, ln_g, ln_b)


def _outproj_kernel(att_ref, ssm_ref, conv_ref, x_ref, g1_ref, lng_ref, lnb_ref, w_ref, o_ref,
                    *, alpha, nk_att, nk_ssm):
    kc = pl.program_id(1)
    w = w_ref[0].astype(BF16)

    @pl.when(kc == 0)
    def _():
        o_ref[...] = jnp.zeros_like(o_ref)

    @pl.when(kc < nk_att)
    def _():
        o_ref[...] += _dot(att_ref[...], w)

    @pl.when((kc >= nk_att) & (kc < nk_att + nk_ssm))
    def _():
        o_ref[...] += _dot(ssm_ref[...], w)

    @pl.when(kc >= nk_att + nk_ssm)
    def _():
        o_ref[...] += _dot(conv_ref[...], w)

    @pl.when(kc == pl.num_programs(1) - 1)
    def _():
        v = alpha * x_ref[...] + _rows(g1_ref) * o_ref[...]
        o_ref[...] = _layer_norm(v, _rows(lng_ref), _rows(lnb_ref))


def _out_proj_ln(att, y_ssm, y_conv, x2d, mod, mod_spec, ln_g, ln_b, w_out, l, alpha, *, tm):
    m, d = x2d.shape
    d_att, d_ssm, d_conv = att.shape[1], y_ssm.shape[1], y_conv.shape[1]
    tk = _tile(math.gcd(d_att, d_ssm, d_conv), 512)
    nk_att, nk_ssm, nk_conv = d_att // tk, d_ssm // tk, d_conv // tk
    nk = nk_att + nk_ssm + nk_conv
    lsel = lambda *shape: pl.BlockSpec((1,) + shape, lambda i, k: (l,) + (0,) * len(shape))
    kernel = functools.partial(_outproj_kernel, alpha=alpha, nk_att=nk_att, nk_ssm=nk_ssm)
    return pl.pallas_call(
        kernel,
        grid=(m // tm, nk),
        in_specs=[
            pl.BlockSpec((tm, tk), lambda i, k: (i, jnp.minimum(k, nk_att - 1))),
            pl.BlockSpec((tm, tk), lambda i, k: (i, jnp.clip(k - nk_att, 0, nk_ssm - 1))),
            pl.BlockSpec((tm, tk), lambda i, k: (i, jnp.clip(k - nk_att - nk_ssm, 0, nk_conv - 1))),
            pl.BlockSpec((tm, d), lambda i, k: (i, 0)),
            mod_spec(2),
            lsel(1, d), lsel(1, d),
            pl.BlockSpec((1, tk, d), lambda i, k: (l, k, 0)),
        ],
        out_specs=pl.BlockSpec((tm, d), lambda i, k: (i, 0)),
        out_shape=jax.ShapeDtypeStruct((m, d), F32),
        compiler_params=_params("parallel", "arbitrary"),
        name="out_proj_ln1",
    )(att, y_ssm, y_conv, x2d, mod, ln_g, ln_b, w_out)


def _ffn_kernel(x_ref, sc_ref, sh_ref, g2_ref, lng_ref, lnb_ref, w1_ref, w2_ref, o_ref, h_scr,
                *, alpha):
    f = pl.program_id(1)

    @pl.when(f == 0)
    def _():
        h = x_ref[...] * (1.0 + _rows(sc_ref)) + _rows(sh_ref)
        h_scr[...] = h.astype(BF16)
        o_ref[...] = jnp.zeros_like(o_ref)

    mid = jnp.maximum(_dot(h_scr[...], w1_ref[0].astype(BF16)), 0.0)
    mid = mid * mid
    o_ref[...] += _dot(mid.astype(BF16), w2_ref[0].astype(BF16))

    @pl.when(f == pl.num_programs(1) - 1)
    def _():
        v = alpha * x_ref[...] + _rows(g2_ref) * o_ref[...]
        o_ref[...] = _layer_norm(v, _rows(lng_ref), _rows(lnb_ref))


def _ffn_ln(x2d, mod, mod_spec, ln_g, ln_b, w_ff1, w_ff2, l, alpha, *, tm):
    m, d = x2d.shape
    d_ff = w_ff1.shape[2]
    tf = _tile(d_ff, 1024)
    lsel = lambda *shape: pl.BlockSpec((1,) + shape, lambda i, f: (l,) + (0,) * len(shape))
    return pl.pallas_call(
        functools.partial(_ffn_kernel, alpha=alpha),
        grid=(m // tm, d_ff // tf),
        in_specs=[
            pl.BlockSpec((tm, d), lambda i, f: (i, 0)),
            mod_spec(4), mod_spec(3), mod_spec(5),
            lsel(1, d), lsel(1, d),
            pl.BlockSpec((1, d, tf), lambda i, f: (l, 0, f)),
            pl.BlockSpec((1, tf, d), lambda i, f: (l, f, 0)),
        ],
        out_specs=pl.BlockSpec((tm, d), lambda i, f: (i, 0)),
        out_shape=jax.ShapeDtypeStruct((m, d), F32),
        scratch_shapes=[pltpu.VMEM((tm, d), BF16)],
        compiler_params=_params("parallel", "arbitrary"),
        name="ffn_ln2",
    )(x2d, mod, mod, mod, ln_g, ln_b, w_ff1, w_ff2)


def kernel(x_prompt, x_sample, cache_k, cache_v, state_ssm_re, state_ssm_im, state_conv, page_table,
           c_prompt, c_sample, w_ada, b_ada, w_in, w_out, sb_bias, ln1_g, ln1_b, ssm_log_dt,
           ssm_lam_re, ssm_lam_im, ssm_b_re, ssm_b_im, ssm_c_re, ssm_c_im, ssm_d, ssm_w_glu,
           ssm_b_glu, conv_w, conv_b, conv_ln_g, conv_ln_b, ln2_g, ln2_b, w_ff1, w_ff2):
    bsz, t, d = x_prompt.shape
    dec_b = x_sample.shape[0]
    depth = w_in.shape[0]
    n_heads = cache_k.shape[3]
    d_att = n_heads * HEAD_DIM
    d_ssm = ssm_d.shape[1]
    d_conv = conv_b.shape[1]
    n_groups, n_state_g = ssm_lam_re.shape[1:]
    n_state = n_groups * n_state_g
    n_pages = page_table.shape[1]
    alpha = (2 * depth) ** 0.25
    assert d_ssm == d_conv and d_att % LANES == 0
    m = bsz * t

    n_rows = -(-(bsz + dec_b) // 8) * 8
    c_all = jnp.concatenate([c_prompt, c_sample, jnp.zeros((n_rows - bsz - dec_b, d), F32)], axis=0)
    mods = _mods(c_all, w_ada, b_ada)
    mod_p = mods[:, :bsz].reshape(depth, bsz, 6, 1, d).transpose(0, 2, 1, 3, 4)
    mod_s = mods[:, bsz:bsz + dec_b].reshape(depth, dec_b, 6, d).transpose(0, 2, 1, 3)

    tm = _tile(t, 512)
    blocks_per_seq = t // tm
    w_in, w_out, w_ff1, w_ff2 = (w.astype(BF16) for w in (w_in, w_out, w_ff1, w_ff2))

    def vec(p):
        return p.reshape(depth, 1, p.shape[-1])

    bt_re = ssm_b_re.transpose(0, 1, 3, 2)
    bt_im = ssm_b_im.transpose(0, 1, 3, 2)
    ab_re, ab_im, bbt_re, bbt_im = _ssm_discretise(ssm_log_dt, ssm_lam_re, ssm_lam_im, bt_re, bt_im)
    bd = jnp.concatenate([_block_diag(bbt_re), _block_diag(bbt_im)], axis=2).astype(BF16)
    cd = jnp.concatenate([_block_diag(ssm_c_re.transpose(0, 1, 3, 2)),
                          _block_diag(-ssm_c_im.transpose(0, 1, 3, 2))], axis=1).astype(BF16)
    a_re = ab_re.reshape(depth, 1, n_state)
    a_im = ab_im.reshape(depth, 1, n_state)
    ssm_d3, b_glu3 = vec(ssm_d), vec(ssm_b_glu)
    conv_b3, cln_g3, cln_b3 = vec(conv_b), vec(conv_ln_g), vec(conv_ln_b)
    ln1_g3, ln1_b3, ln2_g3, ln2_b3 = vec(ln1_g), vec(ln1_b), vec(ln2_g), vec(ln2_b)

    cache_kt = cache_k.transpose(0, 1, 3, 4, 2)
    cache_vt = cache_v.transpose(0, 1, 3, 4, 2)
    pt_flat = page_table.reshape(-1).astype(jnp.int32)
    bias_flat = sb_bias.reshape(-1)
    bias_col = sb_bias.reshape(depth, n_heads, 1)
    h_re_all = state_ssm_re.reshape(depth, dec_b, n_state)
    h_im_all = state_ssm_im.reshape(depth, dec_b, n_state)
    conv_buf_t = state_conv.transpose(0, 2, 1, 3)
    u_tail = _tail_matrix()

    xp = x_prompt.reshape(m, d)
    xs = x_sample.reshape(dec_b, d)
    rest = d_ssm + 2 * d_conv
    kp_l, vp_l, srp_l, sip_l, cvp_l = [], [], [], [], []
    ks_l, vs_l, srs_l, sis_l, cvs_l = [], [], [], [], []
    for l in range(depth):
        def spec_p(k, l=l):
            return pl.BlockSpec((1, 1, 1, 1, d), lambda i, j: (l, k, i // blocks_per_seq, 0, 0))

        def spec_s(k, l=l):
            return pl.BlockSpec((1, 1, dec_b, d), lambda i, j: (l, k, 0, 0))

        q = _in_proj(xp, mod_p, spec_p, w_in, l, 0, d_att, tm=tm, rows_per_batch=t, out_dtype=BF16)
        kvt = _in_proj(xp, mod_p, spec_p, w_in, l, d_att, 2 * d_att, tm=tm, rows_per_batch=t,
                       transpose_out=True)
        proj3 = _in_proj(xp, mod_p, spec_p, w_in, l, 3 * d_att, rest, tm=tm, rows_per_batch=t)
        proj3 = proj3.reshape(bsz, t, rest)
        att = _attn_prompt(q, kvt, bias_flat, u_tail, l, n_heads)
        y_ssm, s_re, s_im = _ssm_prompt(proj3, 0, bd, cd, a_re, a_im, ssm_d3, ssm_w_glu, b_glu3, l)
        y_conv, cbuf = _conv_prompt(proj3, d_ssm // d_conv, conv_w, conv_b3, cln_g3, cln_b3, l)
        x1 = _out_proj_ln(att, y_ssm.reshape(m, d_ssm), y_conv.reshape(m, d_conv), xp, mod_p, spec_p,
                          ln1_g3, ln1_b3, w_out, l, alpha, tm=tm)
        xp = _ffn_ln(x1, mod_p, spec_p, ln2_g3, ln2_b3, w_ff1, w_ff2, l, alpha, tm=tm)
        kp_l.append(kvt[:, :d_att]); vp_l.append(kvt[:, d_att:])
        srp_l.append(s_re); sip_l.append(s_im); cvp_l.append(cbuf)

        proj_s = _in_proj(xs, mod_s, spec_s, w_in, l, 0, 3 * d_att + rest, tm=dec_b,
                          rows_per_batch=1)
        q_s = proj_s[:, :d_att].reshape(dec_b, n_heads, HEAD_DIM, 1)
        att_s = _attn_decode(q_s, cache_kt, cache_vt, pt_flat, bias_col, u_tail, l, n_pages)
        ys_ssm, ss_re, ss_im, ys_conv, nbuf = _sample_mix(
            proj_s[:, 3 * d_att:], h_re_all, h_im_all, conv_buf_t, bd, cd, a_re, a_im, ssm_d3,
            ssm_w_glu, b_glu3, conv_w, conv_b3, cln_g3, cln_b3, l)
        xs1 = _out_proj_ln(att_s.reshape(dec_b, d_att).astype(BF16), ys_ssm, ys_conv, xs, mod_s,
                           spec_s, ln1_g3, ln1_b3, w_out, l, alpha, tm=dec_b)
        xs = _ffn_ln(xs1, mod_s, spec_s, ln2_g3, ln2_b3, w_ff1, w_ff2, l, alpha, tm=dec_b)
        ks_l.append(proj_s[:, d_att:2 * d_att]); vs_l.append(proj_s[:, 2 * d_att:3 * d_att])
        srs_l.append(ss_re); sis_l.append(ss_im); cvs_l.append(nbuf)

    def heads_t(xs_):
        return jnp.stack(xs_).reshape(depth, bsz, n_heads, HEAD_DIM, t).transpose(0, 1, 4, 2, 3)

    def state(xs_, b):
        return jnp.stack(xs_).reshape(depth, b, n_groups, n_state_g)

    return (xp.reshape(bsz, t, d), xs.reshape(dec_b, 1, d),
            heads_t(kp_l), heads_t(vp_l), state(srp_l, bsz), state(sip_l, bsz), jnp.stack(cvp_l),
            jnp.stack(ks_l).reshape(depth, dec_b, 1, n_heads, HEAD_DIM),
            jnp.stack(vs_l).reshape(depth, dec_b, 1, n_heads, HEAD_DIM),
            state(srs_l, dec_b), state(sis_l, dec_b),
            jnp.stack(cvs_l).transpose(0, 2, 1, 3))
```

```python
import functools
import math

import jax
import jax.numpy as jnp
from jax import lax
from jax.experimental import pallas as pl
from jax.experimental.pallas import tpu as pltpu

F32 = jnp.float32
BF16 = jnp.bfloat16

HEAD_DIM = 64
SSM_GROUP = 16
SSM_STATE = 64
LN_EPS = 1e-5
LANES = 128
VMEM_LIMIT_BYTES = 56 * 1024 * 1024


def _params(*sem):
    return pltpu.CompilerParams(dimension_semantics=sem, vmem_limit_bytes=VMEM_LIMIT_BYTES)


def _tile(n, pref):
    t = min(n, pref)
    while n % t:
        t -= 1
    return t


def _rows(ref):
    v = ref[...]
    return v.reshape(v.shape[-2], v.shape[-1])


def _layer_norm(v, g, b):
    mu = jnp.mean(v, axis=-1, keepdims=True)
    d = v - mu
    var = jnp.mean(d * d, axis=-1, keepdims=True)
    return d * lax.rsqrt(var + LN_EPS) * g + b


def _dot(a, b):
    return jnp.dot(a, b, preferred_element_type=F32)


def _dot_nt(a, b):
    return lax.dot_general(a, b, (((1,), (1,)), ((), ())), preferred_element_type=F32)


def _mods_kernel(c_ref, w_ref, b_ref, o_ref):
    c = c_ref[...]
    s = c * jax.nn.sigmoid(c)
    o_ref[0] = _dot(s.astype(BF16), w_ref[0].astype(BF16)) + b_ref[0]


def _mods(c_all, w_ada, b_ada):
    depth, d, n = w_ada.shape
    r = c_all.shape[0]
    tn = _tile(n, 1024)
    return pl.pallas_call(
        _mods_kernel,
        grid=(depth, n // tn),
        in_specs=[
            pl.BlockSpec((r, d), lambda l, j: (0, 0)),
            pl.BlockSpec((1, d, tn), lambda l, j: (l, 0, j)),
            pl.BlockSpec((1, 1, tn), lambda l, j: (l, 0, j)),
        ],
        out_specs=pl.BlockSpec((1, r, tn), lambda l, j: (l, 0, j)),
        out_shape=jax.ShapeDtypeStruct((depth, r, n), F32),
        compiler_params=_params("parallel", "parallel"),
        name="ada_mods",
    )(c_all, w_ada, b_ada.reshape(depth, 1, n))


def _inproj_kernel(x_ref, sc_ref, sh_ref, w_ref, o_ref, h_scr, *, transpose_out):
    @pl.when(pl.program_id(1) == 0)
    def _():
        h = x_ref[...] * (1.0 + _rows(sc_ref)) + _rows(sh_ref)
        h_scr[...] = h.astype(BF16)

    acc = _dot(h_scr[...], w_ref[0].astype(BF16))
    if transpose_out:
        o_ref[0] = acc.T
    else:
        o_ref[...] = acc.astype(o_ref.dtype)


def _in_proj(x2d, mod, mod_spec, w_in, l, col0, ncols, *, tm, rows_per_batch, out_dtype=F32,
             transpose_out=False):
    m, d = x2d.shape
    tn = _tile(math.gcd(col0, ncols), 512)
    jb = col0 // tn
    nb = rows_per_batch // tm if transpose_out else None
    if transpose_out:
        out_shape = jax.ShapeDtypeStruct((m // rows_per_batch, ncols, rows_per_batch), F32)
        out_spec = pl.BlockSpec((1, tn, tm), lambda i, j: (i // nb, j, i % nb))
    else:
        out_shape = jax.ShapeDtypeStruct((m, ncols), out_dtype)
        out_spec = pl.BlockSpec((tm, tn), lambda i, j: (i, j))
    return pl.pallas_call(
        functools.partial(_inproj_kernel, transpose_out=transpose_out),
        grid=(m // tm, ncols // tn),
        in_specs=[
            pl.BlockSpec((tm, d), lambda i, j: (i, 0)),
            mod_spec(1),
            mod_spec(0),
            pl.BlockSpec((1, d, tn), lambda i, j: (l, 0, jb + j)),
        ],
        out_specs=out_spec,
        out_shape=out_shape,
        scratch_shapes=[pltpu.VMEM((tm, d), BF16)],
        compiler_params=_params("parallel", "arbitrary"),
        name="in_proj",
    )(x2d, mod, mod, w_in)


def _sb_logs(z):
    t = jnp.log(1.0 + jnp.exp(-jnp.abs(z)))
    return jnp.minimum(z, 0.0) - t, jnp.maximum(z, 0.0) + t


def _tail_sums(x, u):
    hi = x.astype(BF16)
    lo = (x - hi.astype(F32)).astype(BF16)
    r = _dot(jnp.concatenate([hi, lo], axis=1), u)
    return r[:, :LANES], r[:, LANES:]


def _tail_matrix():
    j = lax.broadcasted_iota(jnp.int32, (2 * LANES, 2 * LANES), 0) % LANES
    s = lax.broadcasted_iota(jnp.int32, (2 * LANES, 2 * LANES), 1)
    return ((j > s) | (s >= LANES)).astype(BF16)


def _attn_kernel(bias_ref, q_ref, k_ref, v_ref, u_ref, o_ref, kb_scr, vb_scr, *, bias0, scale):
    hp = pl.program_id(1)
    qi = pl.program_id(2)
    tq = q_ref.shape[0]
    heads = LANES // HEAD_DIM

    @pl.when(qi == 0)
    def _():
        kb_scr[...] = k_ref[0].astype(BF16)
        vb_scr[...] = v_ref[0].astype(BF16)

    u = u_ref[...]
    rows = [slice(h * HEAD_DIM, (h + 1) * HEAD_DIM) for h in range(heads)]
    qs = [(q_ref[:, r].astype(F32) * scale).astype(BF16) for r in rows]
    biases = [bias_ref[bias0 + hp * heads + h] for h in range(heads)]

    nb = tq // LANES

    def span(ks, cs, mask):
        off = pl.multiple_of(ks * tq, tq)
        lbs, sps = [], []
        for h in range(heads):
            lb, sp = _sb_logs(_dot(qs[h], kb_scr[rows[h], pl.ds(off, tq)]) + biases[h])
            lbs.append(lb)
            sps.append(sp if mask is None else jnp.where(mask, sp, 0.0))
        lane_blk = lambda x, k: x[:, k * LANES:(k + 1) * LANES]
        stacked = jnp.concatenate([lane_blk(sps[h], k) for h in range(heads) for k in range(nb)], axis=0)
        tail, total = _tail_sums(stacked, u)
        out = []
        for h in range(heads):
            c = cs[h]
            ws = [None] * nb
            for k in reversed(range(nb)):
                r0 = (h * nb + k) * tq
                ws[k] = jnp.exp(lane_blk(lbs[h], k) - tail[r0:r0 + tq] - c)
                c = c + total[r0:r0 + tq]
            w = jnp.concatenate(ws, axis=1)
            if mask is not None:
                w = jnp.where(mask, w, 0.0)
            out.append((_dot_nt(w.astype(BF16), vb_scr[rows[h], pl.ds(off, tq)]), c))
        return out

    causal = (lax.broadcasted_iota(jnp.int32, (tq, tq), 1)
              < lax.broadcasted_iota(jnp.int32, (tq, tq), 0))
    zero_c = jnp.zeros((tq, LANES), F32)
    state = tuple(x for pair in span(qi, [zero_c] * heads, causal) for x in pair)

    def body(p, carry):
        res = span(qi - 1 - p, [carry[2 * h + 1] for h in range(heads)], None)
        return tuple(x for h in range(heads) for x in (carry[2 * h] + res[h][0], res[h][1]))

    state = lax.fori_loop(0, qi, body, state)
    for h in range(heads):
        o_ref[:, rows[h]] = state[2 * h].astype(o_ref.dtype)


def _attn_prompt(q, kvt, bias_flat, u, l, n_heads):
    m, d_att = q.shape
    bsz, _, t = kvt.shape
    tq = _tile(t, 512)
    assert tq % LANES == 0
    nq = t // tq
    hp_n = d_att // LANES
    kernel = functools.partial(_attn_kernel, bias0=l * n_heads, scale=HEAD_DIM ** -0.5)
    return pl.pallas_call(
        kernel,
        grid_spec=pltpu.PrefetchScalarGridSpec(
            num_scalar_prefetch=1,
            grid=(bsz, hp_n, nq),
            in_specs=[
                pl.BlockSpec((tq, LANES), lambda b, h, i, s: (b * nq + i, h)),
                pl.BlockSpec((1, LANES, t), lambda b, h, i, s: (b, h, 0)),
                pl.BlockSpec((1, LANES, t), lambda b, h, i, s: (b, hp_n + h, 0)),
                pl.BlockSpec((2 * LANES, 2 * LANES), lambda b, h, i, s: (0, 0)),
            ],
            out_specs=pl.BlockSpec((tq, LANES), lambda b, h, i, s: (b * nq + i, h)),
            scratch_shapes=[pltpu.VMEM((LANES, t), BF16), pltpu.VMEM((LANES, t), BF16)],
        ),
        out_shape=jax.ShapeDtypeStruct((m, d_att), BF16),
        compiler_params=_params("parallel", "parallel", "arbitrary"),
        name="sb_attn_prompt",
    )(bias_flat, q, kvt, kvt, u)


def _decode_kernel(pt_ref, q_ref, bias_ref, u_ref, *refs, pages, scale):
    k_refs = refs[:pages]
    v_refs = refs[pages:2 * pages]
    o_ref, acc_scr, c_scr = refs[2 * pages:]
    s = pl.program_id(1)

    @pl.when(s == 0)
    def _():
        acc_scr[...] = jnp.zeros_like(acc_scr)
        c_scr[...] = jnp.zeros_like(c_scr)

    q = q_ref[0]
    bias = bias_ref[0]
    u = u_ref[...]
    acc = acc_scr[...]
    c = c_scr[...]
    for r in range(pages):
        kt = k_refs[r][0, 0]
        vt = v_refs[r][0, 0]
        z = jnp.sum(kt * q, axis=1) * scale + bias
        lb, sp = _sb_logs(z)
        tail, total = _tail_sums(sp, u)
        w = jnp.exp(lb - tail - c)
        acc = acc + w[:, None, :] * vt
        c = c + total
    acc_scr[...] = acc
    c_scr[...] = c

    @pl.when(s == pl.num_programs(1) - 1)
    def _():
        o_ref[0] = jnp.sum(acc, axis=2)


def _attn_decode(q4, cache_kt, cache_vt, page_table_flat, bias_col, u, l, n_pages):
    bsz, n_heads = q4.shape[:2]
    page = cache_kt.shape[-1]
    assert page == LANES
    pages = _tile(n_pages, 8)
    steps = n_pages // pages

    def page_spec(r):
        def imap(b, s, pt):
            return (l, pt[b * n_pages + (n_pages - 1 - (s * pages + r))], 0, 0, 0)
        return pl.BlockSpec((1, 1, n_heads, HEAD_DIM, page), imap)

    kernel = functools.partial(_decode_kernel, pages=pages, scale=HEAD_DIM ** -0.5)
    return pl.pallas_call(
        kernel,
        grid_spec=pltpu.PrefetchScalarGridSpec(
            num_scalar_prefetch=1,
            grid=(bsz, steps),
            in_specs=[
                pl.BlockSpec((1, n_heads, HEAD_DIM, 1), lambda b, s, pt: (b, 0, 0, 0)),
                pl.BlockSpec((1, n_heads, 1), lambda b, s, pt: (l, 0, 0)),
                pl.BlockSpec((2 * LANES, 2 * LANES), lambda b, s, pt: (0, 0)),
            ] + [page_spec(r) for r in range(pages)] * 2,
            out_specs=pl.BlockSpec((1, n_heads, HEAD_DIM), lambda b, s, pt: (b, 0, 0)),
            scratch_shapes=[pltpu.VMEM((n_heads, HEAD_DIM, page), F32),
                            pltpu.VMEM((n_heads, LANES), F32)],
        ),
        out_shape=jax.ShapeDtypeStruct((bsz, n_heads, HEAD_DIM), F32),
        compiler_params=_params("parallel", "arbitrary"),
        name="sb_attn_decode",
    )(page_table_flat, q4, bias_col, u, *([cache_kt] * pages), *([cache_vt] * pages))


def _ssm_disc_kernel(ldt_ref, lr_ref, li_ref, bre_ref, bim_ref, are_ref, aim_ref, bbre_ref, bbim_ref):
    dt = jnp.exp(ldt_ref[...])
    lr = lr_ref[...]
    li = li_ref[...]
    mag = jnp.exp(lr * dt)
    ang = li * dt
    ab_re = mag * jnp.cos(ang)
    ab_im = mag * jnp.sin(ang)
    den = lr * lr + li * li
    nr = ab_re - 1.0
    f_re = (nr * lr + ab_im * li) / den
    f_im = (ab_im * lr - nr * li) / den
    are_ref[...] = ab_re
    aim_ref[...] = ab_im
    b_re = bre_ref[...]
    b_im = bim_ref[...]
    f_re = f_re[:, :, None, :]
    f_im = f_im[:, :, None, :]
    bbre_ref[...] = f_re * b_re - f_im * b_im
    bbim_ref[...] = f_re * b_im + f_im * b_re


def _ssm_discretise(log_dt, lam_re, lam_im, bt_re, bt_im):
    depth, g, p = lam_re.shape
    return pl.pallas_call(
        _ssm_disc_kernel,
        out_shape=[jax.ShapeDtypeStruct((depth, g, p), F32)] * 2
        + [jax.ShapeDtypeStruct(bt_re.shape, F32)] * 2,
        name="ssm_discretise",
    )(log_dt.reshape(depth, g, 1), lam_re, lam_im, bt_re, bt_im)


def _block_diag(blocks):
    depth, g, r, c = blocks.shape
    eye = jnp.eye(g, dtype=blocks.dtype)
    return (blocks[:, :, :, None, :] * eye[None, :, None, :, None]).reshape(depth, g * r, g * c)


def _ssm_outputs(x_re_im, u2, cd_ref, d_ref, wg_ref, bg_ref):
    y = _dot(x_re_im.astype(BF16), cd_ref[0]) + _rows(d_ref) * u2
    y = jax.nn.gelu(y, approximate=True)
    gate = jax.nn.sigmoid(_dot(y.astype(BF16), wg_ref[0].astype(BF16)) + _rows(bg_ref))
    return y * gate


def _ssm_prompt_kernel(u_ref, bd_ref, cd_ref, are_ref, aim_ref, d_ref, wg_ref, bg_ref,
                       y_ref, sre_ref, sim_ref, bu_scr, st_re, st_im, *, lane_group):
    c = pl.program_id(0)
    bsz, tc, d_ssm = u_ref.shape
    n_state = st_re.shape[1]

    @pl.when(c == 0)
    def _():
        st_re[...] = jnp.zeros_like(st_re)
        st_im[...] = jnp.zeros_like(st_im)

    u2 = u_ref[...].reshape(bsz * tc, d_ssm)
    n_tiles = n_state // LANES
    bu = _dot(u2.astype(BF16), bd_ref[0])
    for k in range(2 * n_tiles):
        bu_scr[k] = bu[:, k * LANES:(k + 1) * LANES]

    for lg in range(n_tiles // lane_group):
        tiles = list(range(lg * lane_group, (lg + 1) * lane_group))
        a_re = [jnp.broadcast_to(are_ref[0, :, k * LANES:(k + 1) * LANES], (bsz, LANES)) for k in tiles]
        a_im = [jnp.broadcast_to(aim_ref[0, :, k * LANES:(k + 1) * LANES], (bsz, LANES)) for k in tiles]

        def step(s, carry, tiles=tiles, a_re=a_re, a_im=a_im):
            x_re, x_im = carry
            rows = pl.ds(s, bsz, stride=tc)
            n_re, n_im = [], []
            for i, k in enumerate(tiles):
                r = a_re[i] * x_re[i] - a_im[i] * x_im[i] + bu_scr[k, rows, :]
                m = a_re[i] * x_im[i] + a_im[i] * x_re[i] + bu_scr[n_tiles + k, rows, :]
                bu_scr[k, rows, :] = r
                bu_scr[n_tiles + k, rows, :] = m
                n_re.append(r)
                n_im.append(m)
            return tuple(n_re), tuple(n_im)

        init = (tuple(st_re[:, k * LANES:(k + 1) * LANES] for k in tiles),
                tuple(st_im[:, k * LANES:(k + 1) * LANES] for k in tiles))
        x_re, x_im = lax.fori_loop(0, tc, step, init, unroll=8)
        for i, k in enumerate(tiles):
            st_re[:, k * LANES:(k + 1) * LANES] = x_re[i]
            st_im[:, k * LANES:(k + 1) * LANES] = x_im[i]

    x = jnp.concatenate([bu_scr[k] for k in range(2 * n_tiles)], axis=1)
    out = _ssm_outputs(x, u2, cd_ref, d_ref, wg_ref, bg_ref)
    y_ref[...] = out.reshape(bsz, tc, d_ssm).astype(y_ref.dtype)

    @pl.when(c == pl.num_programs(0) - 1)
    def _():
        sre_ref[...] = st_re[...]
        sim_ref[...] = st_im[...]


def _ssm_prompt(proj3, u_col, bd, cd, a_re, a_im, ssm_d, w_glu, b_glu, l):
    bsz, t, _ = proj3.shape
    d_ssm = bd.shape[1]
    n_state = a_re.shape[-1]
    tc = _tile(t, 256)
    lane_group = _tile(n_state // LANES, 4)
    lsel = lambda *shape: pl.BlockSpec((1,) + shape, lambda c: (l,) + (0,) * len(shape))
    return pl.pallas_call(
        functools.partial(_ssm_prompt_kernel, lane_group=lane_group),
        grid=(t // tc,),
        in_specs=[
            pl.BlockSpec((bsz, tc, d_ssm), lambda c: (0, c, u_col)),
            lsel(d_ssm, 2 * n_state), lsel(2 * n_state, d_ssm),
            lsel(1, n_state), lsel(1, n_state),
            lsel(1, d_ssm), lsel(d_ssm, d_ssm), lsel(1, d_ssm),
        ],
        out_specs=[
            pl.BlockSpec((bsz, tc, d_ssm), lambda c: (0, c, 0)),
            pl.BlockSpec((bsz, n_state), lambda c: (0, 0)),
            pl.BlockSpec((bsz, n_state), lambda c: (0, 0)),
        ],
        out_shape=[jax.ShapeDtypeStruct((bsz, t, d_ssm), BF16),
                   jax.ShapeDtypeStruct((bsz, n_state), F32),
                   jax.ShapeDtypeStruct((bsz, n_state), F32)],
        scratch_shapes=[pltpu.VMEM((2 * n_state // LANES, bsz * tc, LANES), F32),
                        pltpu.VMEM((bsz, n_state), F32),
                        pltpu.VMEM((bsz, n_state), F32)],
        compiler_params=_params("arbitrary"),
        name="ssm_prompt",
    )(proj3, bd, cd, a_re, a_im, ssm_d, w_glu, b_glu)


def _conv_finish(acc, g_ref, b_ref):
    y = _layer_norm(acc, _rows(g_ref), _rows(b_ref))
    return y * jax.nn.sigmoid(y)


def _conv_prompt_kernel(ca_ref, cb_ref, cw_ref, cbias_ref, g_ref, b_ref, y_ref, buf_ref, gp_scr,
                        *, hist):
    c = pl.program_id(1)
    tc = ca_ref.shape[1]
    width = cw_ref.shape[1]
    pad = hist - (width - 1)

    @pl.when(c == 0)
    def _():
        gp_scr[0:hist, :] = jnp.zeros((hist, gp_scr.shape[1]), F32)

    gp_scr[hist:hist + tc, :] = ca_ref[0] * jax.nn.sigmoid(cb_ref[0])
    acc = _rows(cbias_ref) + gp_scr[pad:pad + tc, :] * cw_ref[0, 0:1, :]
    for w in range(1, width):
        acc = acc + gp_scr[pad + w:pad + w + tc, :] * cw_ref[0, w:w + 1, :]
    y_ref[0] = _conv_finish(acc, g_ref, b_ref).astype(y_ref.dtype)

    @pl.when(c == pl.num_programs(1) - 1)
    def _():
        buf_ref[0] = gp_scr[tc + pad:tc + hist, :]

    gp_scr[0:hist, :] = gp_scr[tc:tc + hist, :]


def _conv_prompt(proj3, ca_col, conv_w, conv_b, ln_g, ln_b, l):
    bsz, t, _ = proj3.shape
    depth, width, d_conv = conv_w.shape
    tc = _tile(t, 256)
    hist = 32
    assert width - 1 <= hist <= tc
    lsel = lambda *shape: pl.BlockSpec((1,) + shape, lambda b, c: (l,) + (0,) * len(shape))
    return pl.pallas_call(
        functools.partial(_conv_prompt_kernel, hist=hist),
        grid=(bsz, t // tc),
        in_specs=[
            pl.BlockSpec((1, tc, d_conv), lambda b, c: (b, c, ca_col)),
            pl.BlockSpec((1, tc, d_conv), lambda b, c: (b, c, ca_col + 1)),
            lsel(width, d_conv), lsel(1, d_conv), lsel(1, d_conv), lsel(1, d_conv),
        ],
        out_specs=[
            pl.BlockSpec((1, tc, d_conv), lambda b, c: (b, c, 0)),
            pl.BlockSpec((1, width - 1, d_conv), lambda b, c: (b, 0, 0)),
        ],
        out_shape=[jax.ShapeDtypeStruct((bsz, t, d_conv), BF16),
                   jax.ShapeDtypeStruct((bsz, width - 1, d_conv), F32)],
        scratch_shapes=[pltpu.VMEM((tc + hist, d_conv), F32)],
        compiler_params=_params("parallel", "arbitrary"),
        name="conv_prompt",
    )(proj3, proj3, conv_w, conv_b, ln_g, ln_b)


def _sample_mix_kernel(u_ref, ca_ref, cb_ref, hre_ref, him_ref, buf_ref,
                       bd_ref, cd_ref, are_ref, aim_ref, d_ref, wg_ref, bg_ref,
                       cw_ref, cbias_ref, g_ref, b_ref,
                       yssm_ref, sre_ref, sim_ref, yconv_ref, nbuf_ref):
    n_state = hre_ref.shape[-1]
    u = u_ref[...]
    bu = _dot(u.astype(BF16), bd_ref[0])
    a_re = are_ref[0]
    a_im = aim_ref[0]
    h_re = hre_ref[0]
    h_im = him_ref[0]
    x_re = bu[:, :n_state] + (a_re * h_re - a_im * h_im)
    x_im = bu[:, n_state:] + (a_re * h_im + a_im * h_re)
    sre_ref[...] = x_re
    sim_ref[...] = x_im
    x = jnp.concatenate([x_re, x_im], axis=1)
    yssm_ref[...] = _ssm_outputs(x, u, cd_ref, d_ref, wg_ref, bg_ref).astype(yssm_ref.dtype)

    width = cw_ref.shape[1]
    g = ca_ref[...] * jax.nn.sigmoid(cb_ref[...])
    acc = _rows(cbias_ref) + g * cw_ref[0, width - 1:width, :]
    for w in range(width - 1):
        acc = acc + buf_ref[0, w] * cw_ref[0, w:w + 1, :]
    yconv_ref[...] = _conv_finish(acc, g_ref, b_ref).astype(yconv_ref.dtype)
    for w in range(width - 2):
        nbuf_ref[w] = buf_ref[0, w + 1]
    nbuf_ref[width - 2] = g


def _sample_mix(proj_s3, h_re, h_im, conv_buf_t, bd, cd, a_re, a_im, ssm_d, w_glu, b_glu,
                conv_w, conv_b, ln_g, ln_b, l):
    bsz = proj_s3.shape[0]
    d_ssm = bd.shape[1]
    n_state = a_re.shape[-1]
    _, width, d_conv = conv_w.shape
    assert d_ssm == d_conv
    lsel = lambda *shape: pl.BlockSpec((1,) + shape, lambda i: (l,) + (0,) * len(shape))
    col = lambda j: pl.BlockSpec((bsz, d_ssm), lambda i: (0, j))
    return pl.pallas_call(
        _sample_mix_kernel,
        grid=(1,),
        in_specs=[
            col(0), col(1), col(2),
            lsel(bsz, n_state), lsel(bsz, n_state), lsel(width - 1, bsz, d_conv),
            lsel(d_ssm, 2 * n_state), lsel(2 * n_state, d_ssm),
            lsel(1, n_state), lsel(1, n_state),
            lsel(1, d_ssm), lsel(d_ssm, d_ssm), lsel(1, d_ssm),
            lsel(width, d_conv), lsel(1, d_conv), lsel(1, d_conv), lsel(1, d_conv),
        ],
        out_specs=[
            pl.BlockSpec((bsz, d_ssm), lambda i: (0, 0)),
            pl.BlockSpec((bsz, n_state), lambda i: (0, 0)),
            pl.BlockSpec((bsz, n_state), lambda i: (0, 0)),
            pl.BlockSpec((bsz, d_conv), lambda i: (0, 0)),
            pl.BlockSpec((width - 1, bsz, d_conv), lambda i: (0, 0, 0)),
        ],
        out_shape=[jax.ShapeDtypeStruct((bsz, d_ssm), BF16),
                   jax.ShapeDtypeStruct((bsz, n_state), F32),
                   jax.ShapeDtypeStruct((bsz, n_state), F32),
                   jax.ShapeDtypeStruct((bsz, d_conv), BF16),
                   jax.ShapeDtypeStruct((width - 1, bsz, d_conv), F32)],
        compiler_params=_params("arbitrary"),
        name="sample_mix",
    )(proj_s3, proj_s3, proj_s3, h_re, h_im, conv_buf_t, bd, cd, a_re, a_im, ssm_d, w_glu, b_glu,
      conv_w, conv_b, ln_g, ln_b)


def _outproj_kernel(att_ref, ssm_ref, conv_ref, x_ref, g1_ref, lng_ref, lnb_ref,
                    wa_ref, ws_ref, wc_ref, o_ref, *, alpha):
    mix = (_dot(att_ref[...], wa_ref[0].astype(BF16))
           + _dot(ssm_ref[...], ws_ref[0].astype(BF16))
           + _dot(conv_ref[...], wc_ref[0].astype(BF16)))
    v = alpha * x_ref[...] + _rows(g1_ref) * mix
    o_ref[...] = _layer_norm(v, _rows(lng_ref), _rows(lnb_ref))


def _out_proj_ln(att, y_ssm, y_conv, x2d, mod, mod_spec, ln_g, ln_b, w_out, l, alpha, *, tm):
    m, d = x2d.shape
    d_att, d_ssm, d_conv = att.shape[1], y_ssm.shape[1], y_conv.shape[1]
    assert d_att % d_ssm == 0 and (d_att + d_ssm) % d_conv == 0
    lsel = lambda *shape: pl.BlockSpec((1,) + shape, lambda i, k: (l,) + (0,) * len(shape))
    return pl.pallas_call(
        functools.partial(_outproj_kernel, alpha=alpha),
        grid=(m // tm, 1),
        in_specs=[
            pl.BlockSpec((tm, d_att), lambda i, k: (i, 0)),
            pl.BlockSpec((tm, d_ssm), lambda i, k: (i, 0)),
            pl.BlockSpec((tm, d_conv), lambda i, k: (i, 0)),
            pl.BlockSpec((tm, d), lambda i, k: (i, 0)),
            mod_spec(2),
            lsel(1, d), lsel(1, d),
            pl.BlockSpec((1, d_att, d), lambda i, k: (l, 0, 0)),
            pl.BlockSpec((1, d_ssm, d), lambda i, k: (l, d_att // d_ssm, 0)),
            pl.BlockSpec((1, d_conv, d), lambda i, k: (l, (d_att + d_ssm) // d_conv, 0)),
        ],
        out_specs=pl.BlockSpec((tm, d), lambda i, k: (i, 0)),
        out_shape=jax.ShapeDtypeStruct((m, d), F32),
        compiler_params=_params("parallel", "arbitrary"),
        name="out_proj_ln1",
    )(att, y_ssm, y_conv, x2d, mod, ln_g, ln_b, w_out, w_out, w_out)


def _ffn_kernel(x_ref, sc_ref, sh_ref, g2_ref, lng_ref, lnb_ref, w1_ref, w2_ref, o_ref, h_scr,
                *, alpha):
    f = pl.program_id(1)

    @pl.when(f == 0)
    def _():
        h = x_ref[...] * (1.0 + _rows(sc_ref)) + _rows(sh_ref)
        h_scr[...] = h.astype(BF16)
        o_ref[...] = jnp.zeros_like(o_ref)

    mid = jnp.maximum(_dot(h_scr[...], w1_ref[0].astype(BF16)), 0.0)
    mid = mid * mid
    o_ref[...] += _dot(mid.astype(BF16), w2_ref[0].astype(BF16))

    @pl.when(f == pl.num_programs(1) - 1)
    def _():
        v = alpha * x_ref[...] + _rows(g2_ref) * o_ref[...]
        o_ref[...] = _layer_norm(v, _rows(lng_ref), _rows(lnb_ref))


def _ffn_ln(x2d, mod, mod_spec, ln_g, ln_b, w_ff1, w_ff2, l, alpha, *, tm):
    m, d = x2d.shape
    d_ff = w_ff1.shape[2]
    tf = _tile(d_ff, 1024)
    lsel = lambda *shape: pl.BlockSpec((1,) + shape, lambda i, f: (l,) + (0,) * len(shape))
    return pl.pallas_call(
        functools.partial(_ffn_kernel, alpha=alpha),
        grid=(m // tm, d_ff // tf),
        in_specs=[
            pl.BlockSpec((tm, d), lambda i, f: (i, 0)),
            mod_spec(4), mod_spec(3), mod_spec(5),
            lsel(1, d), lsel(1, d),
            pl.BlockSpec((1, d, tf), lambda i, f: (l, 0, f)),
            pl.BlockSpec((1, tf, d), lambda i, f: (l, f, 0)),
        ],
        out_specs=pl.BlockSpec((tm, d), lambda i, f: (i, 0)),
        out_shape=jax.ShapeDtypeStruct((m, d), F32),
        scratch_shapes=[pltpu.VMEM((tm, d), BF16)],
        compiler_params=_params("parallel", "arbitrary"),
        name="ffn_ln2",
    )(x2d, mod, mod, mod, ln_g, ln_b, w_ff1, w_ff2)


def kernel(x_prompt, x_sample, cache_k, cache_v, state_ssm_re, state_ssm_im, state_conv, page_table,
           c_prompt, c_sample, w_ada, b_ada, w_in, w_out, sb_bias, ln1_g, ln1_b, ssm_log_dt,
           ssm_lam_re, ssm_lam_im, ssm_b_re, ssm_b_im, ssm_c_re, ssm_c_im, ssm_d, ssm_w_glu,
           ssm_b_glu, conv_w, conv_b, conv_ln_g, conv_ln_b, ln2_g, ln2_b, w_ff1, w_ff2):
    bsz, t, d = x_prompt.shape
    dec_b = x_sample.shape[0]
    depth = w_in.shape[0]
    n_heads = cache_k.shape[3]
    d_att = n_heads * HEAD_DIM
    d_ssm = ssm_d.shape[1]
    d_conv = conv_b.shape[1]
    n_groups, n_state_g = ssm_lam_re.shape[1:]
    n_state = n_groups * n_state_g
    n_pages = page_table.shape[1]
    alpha = (2 * depth) ** 0.25
    assert d_ssm == d_conv and d_att % LANES == 0
    m = bsz * t

    n_rows = -(-(bsz + dec_b) // 8) * 8
    c_all = jnp.concatenate([c_prompt, c_sample, jnp.zeros((n_rows - bsz - dec_b, d), F32)], axis=0)
    mods = _mods(c_all, w_ada, b_ada)
    mod_p = mods[:, :bsz].reshape(depth, bsz, 6, 1, d).transpose(0, 2, 1, 3, 4)
    mod_s = mods[:, bsz:bsz + dec_b].reshape(depth, dec_b, 6, d).transpose(0, 2, 1, 3)

    tm = _tile(t, 512)
    blocks_per_seq = t // tm
    w_in, w_out, w_ff1, w_ff2 = (w.astype(BF16) for w in (w_in, w_out, w_ff1, w_ff2))

    def vec(p):
        return p.reshape(depth, 1, p.shape[-1])

    bt_re = ssm_b_re.transpose(0, 1, 3, 2)
    bt_im = ssm_b_im.transpose(0, 1, 3, 2)
    ab_re, ab_im, bbt_re, bbt_im = _ssm_discretise(ssm_log_dt, ssm_lam_re, ssm_lam_im, bt_re, bt_im)
    bd = jnp.concatenate([_block_diag(bbt_re), _block_diag(bbt_im)], axis=2).astype(BF16)
    cd = jnp.concatenate([_block_diag(ssm_c_re.transpose(0, 1, 3, 2)),
                          _block_diag(-ssm_c_im.transpose(0, 1, 3, 2))], axis=1).astype(BF16)
    a_re = ab_re.reshape(depth, 1, n_state)
    a_im = ab_im.reshape(depth, 1, n_state)
    ssm_d3, b_glu3 = vec(ssm_d), vec(ssm_b_glu)
    conv_b3, cln_g3, cln_b3 = vec(conv_b), vec(conv_ln_g), vec(conv_ln_b)
    ln1_g3, ln1_b3, ln2_g3, ln2_b3 = vec(ln1_g), vec(ln1_b), vec(ln2_g), vec(ln2_b)

    cache_kt = cache_k.transpose(0, 1, 3, 4, 2)
    cache_vt = cache_v.transpose(0, 1, 3, 4, 2)
    pt_flat = page_table.reshape(-1).astype(jnp.int32)
    bias_flat = sb_bias.reshape(-1)
    bias_col = sb_bias.reshape(depth, n_heads, 1)
    h_re_all = state_ssm_re.reshape(depth, dec_b, n_state)
    h_im_all = state_ssm_im.reshape(depth, dec_b, n_state)
    conv_buf_t = state_conv.transpose(0, 2, 1, 3)
    u_tail = _tail_matrix()

    xp = x_prompt.reshape(m, d)
    xs = x_sample.reshape(dec_b, d)
    rest = d_ssm + 2 * d_conv
    kp_l, vp_l, srp_l, sip_l, cvp_l = [], [], [], [], []
    ks_l, vs_l, srs_l, sis_l, cvs_l = [], [], [], [], []
    for l in range(depth):
        def spec_p(k, l=l):
            return pl.BlockSpec((1, 1, 1, 1, d), lambda i, j: (l, k, i // blocks_per_seq, 0, 0))

        def spec_s(k, l=l):
            return pl.BlockSpec((1, 1, dec_b, d), lambda i, j: (l, k, 0, 0))

        q = _in_proj(xp, mod_p, spec_p, w_in, l, 0, d_att, tm=tm, rows_per_batch=t, out_dtype=BF16)
        kvt = _in_proj(xp, mod_p, spec_p, w_in, l, d_att, 2 * d_att, tm=tm, rows_per_batch=t,
                       transpose_out=True)
        proj3 = _in_proj(xp, mod_p, spec_p, w_in, l, 3 * d_att, rest, tm=tm, rows_per_batch=t)
        proj3 = proj3.reshape(bsz, t, rest)
        att = _attn_prompt(q, kvt, bias_flat, u_tail, l, n_heads)
        y_ssm, s_re, s_im = _ssm_prompt(proj3, 0, bd, cd, a_re, a_im, ssm_d3, ssm_w_glu, b_glu3, l)
        y_conv, cbuf = _conv_prompt(proj3, d_ssm // d_conv, conv_w, conv_b3, cln_g3, cln_b3, l)
        x1 = _out_proj_ln(att, y_ssm.reshape(m, d_ssm), y_conv.reshape(m, d_conv), xp, mod_p, spec_p,
                          ln1_g3, ln1_b3, w_out, l, alpha, tm=tm)
        xp = _ffn_ln(x1, mod_p, spec_p, ln2_g3, ln2_b3, w_ff1, w_ff2, l, alpha, tm=tm)
        kp_l.append(kvt[:, :d_att]); vp_l.append(kvt[:, d_att:])
        srp_l.append(s_re); sip_l.append(s_im); cvp_l.append(cbuf)

        proj_s = _in_proj(xs, mod_s, spec_s, w_in, l, 0, 3 * d_att + rest, tm=dec_b,
                          rows_per_batch=1)
        q_s = proj_s[:, :d_att].reshape(dec_b, n_heads, HEAD_DIM, 1)
        att_s = _attn_decode(q_s, cache_kt, cache_vt, pt_flat, bias_col, u_tail, l, n_pages)
        ys_ssm, ss_re, ss_im, ys_conv, nbuf = _sample_mix(
            proj_s[:, 3 * d_att:], h_re_all, h_im_all, conv_buf_t, bd, cd, a_re, a_im, ssm_d3,
            ssm_w_glu, b_glu3, conv_w, conv_b3, cln_g3, cln_b3, l)
        xs1 = _out_proj_ln(att_s.reshape(dec_b, d_att).astype(BF16), ys_ssm, ys_conv, xs, mod_s,
                           spec_s, ln1_g3, ln1_b3, w_out, l, alpha, tm=dec_b)
        xs = _ffn_ln(xs1, mod_s, spec_s, ln2_g3, ln2_b3, w_ff1, w_ff2, l, alpha, tm=dec_b)
        ks_l.append(proj_s[:, d_att:2 * d_att]); vs_l.append(proj_s[:, 2 * d_att:3 * d_att])
        srs_l.append(ss_re); sis_l.append(ss_im); cvs_l.append(nbuf)

    def heads_t(xs_):
        return jnp.stack(xs_).reshape(depth, bsz, n_heads, HEAD_DIM, t).transpose(0, 1, 4, 2, 3)

    def state(xs_, b):
        return jnp.stack(xs_).reshape(depth, b, n_groups, n_state_g)

    return (xp.reshape(bsz, t, d), xs.reshape(dec_b, 1, d),
            heads_t(kp_l), heads_t(vp_l), state(srp_l, bsz), state(sip_l, bsz), jnp.stack(cvp_l),
            jnp.stack(ks_l).reshape(depth, dec_b, 1, n_heads, HEAD_DIM),
            jnp.stack(vs_l).reshape(depth, dec_b, 1, n_heads, HEAD_DIM),
            state(srs_l, dec_b), state(sis_l, dec_b),
            jnp.stack(cvs_l).transpose(0, 2, 1, 3))
```
